```python
import jax
import jax.numpy as jnp
from jax import lax
import numpy as np

D_MODEL = 1024
BATCH = 32
SEQ = 2048
DEPTH = 4

GRID_W = 64
CTX_LEN = 256
EPS = 1e-6
NEG = -1e30
ROPE_THETA = 10000.0
Q_BLOCK = 128
CHUNK = 64

A_HEADS = 4
A_DK = 128
A_DV = 128
B_HEADS = 4
B_DQK = 64
B_DV = 128
B_CONV = 3
C_HEADS = 8
C_KV_HEADS = 2
C_DH = 64
D_HEADS = 4
D_Q_LORA = 256
D_KV_LORA = 128
D_NOPE = 128
D_ROPE = 64
D_DV = 128
N_BRANCH = 4
BRANCH_W = 512
D_FF = 4 * D_MODEL

KEY_COLS = (
    ('a_i', A_HEADS * A_DV),
    ('a_f_fwd', A_HEADS * A_DK),
    ('a_f_bwd', A_HEADS * A_DK),
    ('b_k', B_HEADS * B_DQK),
    ('b_v', B_HEADS * B_DV),
    ('b_gates', 4 * B_HEADS),
    ('c_k', C_KV_HEADS * C_DH),
    ('c_v', C_KV_HEADS * C_DH),
    ('d_ckv', D_KV_LORA),
    ('d_krope', D_ROPE),
)
QUERY_COLS = (
    ('a_q', A_HEADS * A_DK),
    ('a_g', A_HEADS * A_DV),
    ('b_q', B_HEADS * B_DQK),
    ('b_o', B_HEADS * B_DV),
    ('c_q', C_HEADS * C_DH),
    ('d_cq', D_Q_LORA),
    ('gates', N_BRANCH * D_MODEL),
)
KEY_WIDTH = sum(w for _, w in KEY_COLS)
IN_WIDTH = KEY_WIDTH + sum(w for _, w in QUERY_COLS)

kernel_name = 'hybrid_parallel_mixer_dit_trunk'


def rms_norm(x, g):
    xf = x.astype(jnp.float32)
    y = xf * lax.rsqrt(jnp.mean(xf * xf, axis=-1, keepdims=True) + EPS)
    return (y * g.astype(jnp.float32)).astype(x.dtype)


def modulate(h, shift, scale):
    return h * (1.0 + scale) + shift


def split_cols(p, layout):
    sizes = [w for _, w in layout]
    parts = jnp.split(p, np.cumsum(sizes)[:-1].tolist(), axis=-1)
    return {name: part for (name, _), part in zip(layout, parts)}


def split_heads(x, h):
    return x.reshape(x.shape[0], x.shape[1], h, -1)


def to_heads(x, h):
    return split_heads(x, h).transpose(0, 2, 1, 3)


def from_heads(x):
    b, h, n, d = x.shape
    return x.transpose(0, 2, 1, 3).reshape(b, n, h * d)


def flip_time(t, rev):
    return jnp.flip(t, axis=2) if rev else t


def to_chunks(x):
    b, h, n = x.shape[:3]
    x = x.reshape(b, h, n // CHUNK, CHUNK, *x.shape[3:])
    return jnp.moveaxis(x, 2, 0)


def from_chunks(x):
    x = jnp.moveaxis(x, 0, 2)
    b, h, nc, l = x.shape[:4]
    return x.reshape(b, h, nc * l, *x.shape[4:])


def axial_rope_tables(row, col, rot_dim):
    quarter = rot_dim // 4
    inv_freq = ROPE_THETA ** (-jnp.arange(quarter, dtype=jnp.float32) / quarter)
    ang_r = row.astype(jnp.float32)[:, None] * inv_freq
    ang_c = col.astype(jnp.float32)[:, None] * inv_freq
    return (jnp.cos(ang_r)[:, None], jnp.sin(ang_r)[:, None], jnp.cos(ang_c)[:, None], jnp.sin(ang_c)[:, None])


def rope_rotate(x, cos, sin):
    x1, x2 = jnp.split(x, 2, axis=-1)
    return jnp.concatenate([x1 * cos - x2 * sin, x2 * cos + x1 * sin], axis=-1)


def rope_2d(x, tabs):
    cr, sr, cc, sc = tabs
    xr, xc = jnp.split(x, 2, axis=-1)
    return jnp.concatenate([rope_rotate(xr, cr, sr), rope_rotate(xc, cc, sc)], axis=-1).astype(x.dtype)


def block_attention(q, k, v, scale):
    b, nq, hk, g, dk = q.shape
    nb = nq // Q_BLOCK
    qb = q.reshape(b, nb, Q_BLOCK, hk, g, dk).transpose(1, 0, 2, 3, 4, 5)

    def one_block(qi):
        s = jnp.einsum('bqhgd,bmhd->bhgqm', qi, k, preferred_element_type=jnp.float32) * scale
        p = jax.nn.softmax(s, axis=-1)
        return jnp.einsum('bhgqm,bmhd->bqhgd', p.astype(v.dtype), v)

    out = lax.map(one_block, qb)
    return out.transpose(1, 0, 2, 3, 4, 5).reshape(b, nq, hk, g, v.shape[-1])


def short_conv(x, w):
    return lax.conv_general_dilated(x, w[:, None, :].astype(x.dtype), window_strides=(1,), padding='SAME',
                                    dimension_numbers=('NWC', 'WIO', 'NWC'), feature_group_count=x.shape[-1])


def hgrn2_gates(f_pre, lb):
    log_f = jnp.logaddexp(jnp.log(lb), jnp.log1p(-lb) + jax.nn.log_sigmoid(f_pre.astype(jnp.float32)))
    return to_heads(-jnp.expm1(log_f), A_HEADS), to_heads(log_f, A_HEADS)


def hgrn2_scan(q, k, v, log_f, s0):
    causal = jnp.tril(jnp.ones((CHUNK, CHUNK), dtype=bool))

    def step(s, inp):
        qc, kc, vc, lfc = inp
        bcum = jnp.cumsum(lfc, axis=2)
        rel = bcum[:, :, :, None, :] - bcum[:, :, None, :, :]
        decay = jnp.exp(jnp.where(causal[:, :, None], rel, NEG))
        a = jnp.einsum('bhtk,bhsk,bhtsk->bhts', qc, kc, decay)
        o = jnp.einsum('bhtk,bhkv->bhtv', qc * jnp.exp(bcum), s) + jnp.einsum('bhts,bhsv->bhtv', a, vc)
        blast = bcum[:, :, -1:, :]
        s_new = jnp.exp(blast[:, :, 0, :])[..., None] * s + jnp.einsum('bhsk,bhsv->bhkv', kc * jnp.exp(blast - bcum), vc)
        return s_new, o

    s_fin, o = lax.scan(step, s0, (to_chunks(q), to_chunks(k), to_chunks(v), to_chunks(log_f)))
    return from_chunks(o), s_fin


def hgrn2_final_state(k, v, log_f):
    bcum = jnp.cumsum(log_f, axis=2)
    return jnp.einsum('bhnk,bhnv->bhkv', k * jnp.exp(bcum[:, :, -1:, :] - bcum), v)


def hgrn2_mixer(lat, ctx, lb, norm_g, need_ctx_out):
    f32 = jnp.float32
    q_l = to_heads(jax.nn.silu(lat['a_q'].astype(f32)), A_HEADS)
    v_l = to_heads(lat['a_i'].astype(f32), A_HEADS)
    v_c = to_heads(ctx['a_i'].astype(f32), A_HEADS)
    q_c = to_heads(jax.nn.silu(ctx['a_q'].astype(f32)), A_HEADS) if need_ctx_out else None
    o_l, o_c = 0.0, 0.0
    for d, name in enumerate(('a_f_fwd', 'a_f_bwd')):
        rev = d == 1
        k_l, lf_l = hgrn2_gates(lat[name], lb[d])
        k_c, lf_c = hgrn2_gates(ctx[name], lb[d])
        if need_ctx_out:
            zero = jnp.zeros((v_c.shape[0], A_HEADS, A_DK, A_DV), f32)
            oc, s_ctx = hgrn2_scan(flip_time(q_c, rev), flip_time(k_c, rev), flip_time(v_c, rev),
                                   flip_time(lf_c, rev), zero)
            o_c = o_c + flip_time(oc, rev)
        else:
            s_ctx = hgrn2_final_state(flip_time(k_c, rev), flip_time(v_c, rev), flip_time(lf_c, rev))
        ol, _ = hgrn2_scan(flip_time(q_l, rev), flip_time(k_l, rev), flip_time(v_l, rev),
                           flip_time(lf_l, rev), s_ctx)
        o_l = o_l + flip_time(ol, rev)

    def readout(o, g):
        return (from_heads(rms_norm(o, norm_g)) * jax.nn.silu(g.astype(f32))).astype(g.dtype)

    return readout(o_l, lat['a_g']), (readout(o_c, ctx['a_g']) if need_ctx_out else None)


def mlstm_scan(q, k, v, ig, lf, state):
    causal = jnp.tril(jnp.ones((CHUNK, CHUNK), dtype=bool))

    def step(carry, inp):
        cmat, nvec, m = carry
        qc, kc, vc, igc, lfc = inp
        b = jnp.cumsum(lfc, axis=-1)
        dmat = jnp.where(causal, b[..., :, None] - b[..., None, :] + igc[..., None, :], NEG)
        inter = b + m[..., None]
        m_t = jnp.maximum(inter, jnp.max(dmat, axis=-1))
        w_inter = jnp.exp(inter - m_t)
        s = jnp.einsum('bhtk,bhsk->bhts', qc, kc) * jnp.exp(dmat - m_t[..., None])
        num = w_inter[..., None] * jnp.einsum('bhtk,bhkv->bhtv', qc, cmat) + jnp.einsum('bhts,bhsv->bhtv', s, vc)
        den = w_inter * jnp.einsum('bhtk,bhk->bht', qc, nvec) + jnp.sum(s, axis=-1)
        h = num / jnp.maximum(jnp.abs(den), jnp.exp(-m_t))[..., None]
        m_new = m_t[..., -1]
        dec = jnp.exp(b[..., -1] + m - m_new)
        wk = jnp.exp(b[..., -1:] - b + igc - m_new[..., None])
        c_new = dec[..., None, None] * cmat + jnp.einsum('bhs,bhsk,bhsv->bhkv', wk, kc, vc)
        n_new = dec[..., None] * nvec + jnp.einsum('bhs,bhsk->bhk', wk, kc)
        return (c_new, n_new, m_new), h

    fin, h = lax.scan(step, state, (to_chunks(q), to_chunks(k), to_chunks(v), to_chunks(ig), to_chunks(lf)))
    return from_chunks(h), fin


def mlstm_final_state(k, v, ig, lf):
    b = jnp.cumsum(lf, axis=-1)
    logw = b[..., -1:] - b + ig
    m = jnp.max(logw, axis=-1)
    w = jnp.exp(logw - m[..., None])
    return (jnp.einsum('bhn,bhnk,bhnv->bhkv', w, k, v), jnp.einsum('bhn,bhnk->bhk', w, k), m)


def mlstm_mixer(lat, ctx, gate_bias, conv_w, norm_g, need_ctx_out):
    f32 = jnp.float32

    def prep(side, with_q):
        b, n = side['b_v'].shape[:2]
        k = to_heads((jax.nn.silu(short_conv(side['b_k'], conv_w[1])) * B_DQK ** -0.5).astype(f32), B_HEADS)
        v = to_heads(side['b_v'].astype(f32), B_HEADS)
        q = to_heads(jax.nn.silu(short_conv(side['b_q'], conv_w[0])).astype(f32), B_HEADS) if with_q else None
        g = (side['b_gates'] + gate_bias).astype(f32).reshape(b, n, 4, B_HEADS).transpose(2, 0, 3, 1)
        return q, k, v, g

    q_l, k_l, v_l, g_l = prep(lat, True)
    q_c, k_c, v_c, g_c = prep(ctx, need_ctx_out)
    h_l, h_c = 0.0, 0.0
    for d in range(2):
        rev = d == 1
        ig_c, lf_c = g_c[2 * d], jax.nn.log_sigmoid(g_c[2 * d + 1])
        ig_l, lf_l = g_l[2 * d], jax.nn.log_sigmoid(g_l[2 * d + 1])
        if need_ctx_out:
            bsz = v_c.shape[0]
            zero = (jnp.zeros((bsz, B_HEADS, B_DQK, B_DV), f32), jnp.zeros((bsz, B_HEADS, B_DQK), f32),
                    jnp.full((bsz, B_HEADS), NEG, f32))
            hc, st = mlstm_scan(flip_time(q_c, rev), flip_time(k_c, rev), flip_time(v_c, rev),
                                flip_time(ig_c, rev), flip_time(lf_c, rev), zero)
            h_c = h_c + flip_time(hc, rev)
        else:
            st = mlstm_final_state(flip_time(k_c, rev), flip_time(v_c, rev), flip_time(ig_c, rev), flip_time(lf_c, rev))
        hl, _ = mlstm_scan(flip_time(q_l, rev), flip_time(k_l, rev), flip_time(v_l, rev),
                           flip_time(ig_l, rev), flip_time(lf_l, rev), st)
        h_l = h_l + flip_time(hl, rev)

    def readout(h, o):
        return (from_heads(rms_norm(h, norm_g)) * jax.nn.sigmoid(o.astype(f32))).astype(o.dtype)

    return readout(h_l, lat['b_o']), (readout(h_c, ctx['b_o']) if need_ctx_out else None)


def gqa_mixer(lat, ctx, q_g, k_g, tabs, need_ctx_out):
    grp = C_HEADS // C_KV_HEADS
    scale = C_DH ** -0.5

    def keys(side, rope):
        k = rms_norm(split_heads(side['c_k'], C_KV_HEADS), k_g)
        if rope is not None:
            k = rope_2d(k, rope)
        return k, split_heads(side['c_v'], C_KV_HEADS)

    def queries(side, rope):
        q = rms_norm(split_heads(side['c_q'], C_HEADS), q_g)
        if rope is not None:
            q = rope_2d(q, rope)
        return q.reshape(q.shape[0], q.shape[1], C_KV_HEADS, grp, C_DH)

    k_c, v_c = keys(ctx, None)
    k_l, v_l = keys(lat, tabs)
    y_l = block_attention(queries(lat, tabs), jnp.concatenate([k_c, k_l], axis=1),
                          jnp.concatenate([v_c, v_l], axis=1), scale)
    y_l = y_l.reshape(y_l.shape[0], y_l.shape[1], -1)
    if not need_ctx_out:
        return y_l, None
    y_c = block_attention(queries(ctx, None), k_c, v_c, scale)
    return y_l, y_c.reshape(y_c.shape[0], y_c.shape[1], -1)


def mla_mixer(lat, ctx, q_g, kv_g, w_uq, w_uk, w_uv, tabs, need_ctx_out):
    scale = (D_NOPE + D_ROPE) ** -0.5

    def keys(side, rope):
        ckv = rms_norm(side['d_ckv'], kv_g)
        k_nope = split_heads(ckv @ w_uk, D_HEADS)
        v = split_heads(ckv @ w_uv, D_HEADS)
        k_rope = side['d_krope'][:, :, None, :]
        if rope is not None:
            k_rope = rope_2d(k_rope, rope)
        k_rope = jnp.broadcast_to(k_rope, k_nope.shape[:3] + (D_ROPE,))
        return jnp.concatenate([k_nope, k_rope], axis=-1), v

    def queries(side, rope):
        q = split_heads(rms_norm(side['d_cq'], q_g) @ w_uq, D_HEADS)
        q_nope, q_rope = q[..., :D_NOPE], q[..., D_NOPE:]
        if rope is not None:
            q_rope = rope_2d(q_rope, rope)
        return jnp.concatenate([q_nope, q_rope], axis=-1)[:, :, :, None, :]

    k_c, v_c = keys(ctx, None)
    k_l, v_l = keys(lat, tabs)
    y_l = block_attention(queries(lat, tabs), jnp.concatenate([k_c, k_l], axis=1),
                          jnp.concatenate([v_c, v_l], axis=1), scale)
    y_l = y_l.reshape(y_l.shape[0], y_l.shape[1], -1)
    if not need_ctx_out:
        return y_l, None
    y_c = block_attention(queries(ctx, None), k_c, v_c, scale)
    return y_l, y_c.reshape(y_c.shape[0], y_c.shape[1], -1)


def merge_branches(ys, gate_pre, w_branch_l):
    g = gate_pre.reshape(*gate_pre.shape[:-1], N_BRANCH, D_MODEL)
    out = 0.0
    for r, y in enumerate(ys):
        out = out + jax.nn.sigmoid(g[..., r, :]) * (y @ w_branch_l[r])
    return out


def squared_relu_mlp(h, w1, w2):
    return jnp.square(jax.nn.relu(h @ w1)) @ w2


def setup_inputs(seed: int = 0) -> dict:
    key = jax.random.key(seed)
    ks = iter(jax.random.split(key, 32))

    def nrm(shape, std):
        return std * jax.random.normal(next(ks), shape, jnp.float32)

    def gain(shape):
        return 1.0 + nrm(shape, 0.02)

    x = nrm((BATCH, SEQ, D_MODEL), 1.0)
    c = nrm((BATCH, D_MODEL), 1.0)
    ctx = nrm((BATCH, CTX_LEN, D_MODEL), 1.0)
    c_ctx = nrm((D_MODEL,), 1.0)
    w_ada = nrm((DEPTH, D_MODEL, 6 * D_MODEL), 0.5 * D_MODEL ** -0.5)
    b_ada = nrm((DEPTH, 6 * D_MODEL), 0.02)
    g_norm1 = gain((DEPTH, D_MODEL))
    g_norm2 = gain((DEPTH, D_MODEL))
    w_in = nrm((DEPTH, D_MODEL, IN_WIDTH), D_MODEL ** -0.5)
    i_bias = nrm((DEPTH, 2, B_HEADS), 0.1)
    f_bias = jnp.linspace(3.0, 6.0, B_HEADS, dtype=jnp.float32) + nrm((DEPTH, 2, B_HEADS), 0.1)
    b_mlstm_gates = jnp.stack([i_bias, f_bias], axis=2).reshape(DEPTH, 4 * B_HEADS)
    hgrn_lb_logits = nrm((DEPTH, 2, A_HEADS * A_DK), 0.5)
    hgrn_norm_g = gain((DEPTH, A_DV))
    mlstm_conv_w = nrm((DEPTH, 2, B_CONV, B_HEADS * B_DQK), B_CONV ** -0.5)
    mlstm_norm_g = gain((DEPTH, B_DV))
    gqa_q_norm_g = gain((DEPTH, C_DH))
    gqa_k_norm_g = gain((DEPTH, C_DH))
    mla_q_norm_g = gain((DEPTH, D_Q_LORA))
    mla_kv_norm_g = gain((DEPTH, D_KV_LORA))
    w_mla_uq = nrm((DEPTH, D_Q_LORA, D_HEADS * (D_NOPE + D_ROPE)), D_Q_LORA ** -0.5)
    w_mla_uk = nrm((DEPTH, D_KV_LORA, D_HEADS * D_NOPE), D_KV_LORA ** -0.5)
    w_mla_uv = nrm((DEPTH, D_KV_LORA, D_HEADS * D_DV), D_KV_LORA ** -0.5)
    w_branch = nrm((DEPTH, N_BRANCH, BRANCH_W, D_MODEL), BRANCH_W ** -0.5)
    w_out = nrm((DEPTH, D_MODEL, D_MODEL), D_MODEL ** -0.5)
    w_ff1 = nrm((DEPTH, D_MODEL, D_FF), D_MODEL ** -0.5)
    w_ff2 = nrm((DEPTH, D_FF, D_MODEL), D_FF ** -0.5)
    g_final = gain((D_MODEL,))
    return {'x': x, 'c': c, 'ctx': ctx, 'c_ctx': c_ctx, 'w_ada': w_ada, 'b_ada': b_ada,
            'g_norm1': g_norm1, 'g_norm2': g_norm2, 'w_in': w_in, 'b_mlstm_gates': b_mlstm_gates,
            'hgrn_lb_logits': hgrn_lb_logits, 'hgrn_norm_g': hgrn_norm_g, 'mlstm_conv_w': mlstm_conv_w,
            'mlstm_norm_g': mlstm_norm_g, 'gqa_q_norm_g': gqa_q_norm_g, 'gqa_k_norm_g': gqa_k_norm_g,
            'mla_q_norm_g': mla_q_norm_g, 'mla_kv_norm_g': mla_kv_norm_g, 'w_mla_uq': w_mla_uq,
            'w_mla_uk': w_mla_uk, 'w_mla_uv': w_mla_uv, 'w_branch': w_branch, 'w_out': w_out,
            'w_ff1': w_ff1, 'w_ff2': w_ff2, 'g_final': g_final}


def reference(x, c, ctx, c_ctx, w_ada, b_ada, g_norm1, g_norm2, w_in, b_mlstm_gates, hgrn_lb_logits,
              hgrn_norm_g, mlstm_conv_w, mlstm_norm_g, gqa_q_norm_g, gqa_k_norm_g, mla_q_norm_g,
              mla_kv_norm_g, w_mla_uq, w_mla_uk, w_mla_uv, w_branch, w_out, w_ff1, w_ff2, g_final):
    n_lat = x.shape[1]
    rows = n_lat // GRID_W
    row = jnp.repeat(jnp.arange(rows, dtype=jnp.int32), GRID_W)
    col = jnp.broadcast_to(jnp.arange(GRID_W, dtype=jnp.int32), (rows, GRID_W)).reshape(-1)
    rope_c = axial_rope_tables(row, col, C_DH)
    rope_d = axial_rope_tables(row, col, D_ROPE)

    lb_all = jnp.cumsum(jax.nn.softmax(hgrn_lb_logits.astype(jnp.float32), axis=0), axis=0)
    lb_all = lb_all - lb_all[0]

    s_c = jax.nn.silu(c)
    s_cc = jax.nn.silu(c_ctx)
    x_l, x_c = x, ctx
    for l in range(DEPTH):
        need_ctx = l < DEPTH - 1
        mod = jnp.split((s_c @ w_ada[l] + b_ada[l])[:, None, :], 6, axis=-1)
        n_cm = 6 if need_ctx else 2
        mod_c = jnp.split(s_cc @ w_ada[l][:, :n_cm * D_MODEL] + b_ada[l][:n_cm * D_MODEL], n_cm)

        h_l = modulate(rms_norm(x_l, g_norm1[l]), mod[0], mod[1])
        h_c = modulate(rms_norm(x_c, g_norm1[l]), mod_c[0], mod_c[1])
        p_l = h_l @ w_in[l]
        lat = split_cols(p_l[..., :KEY_WIDTH], KEY_COLS)
        lat.update(split_cols(p_l[..., KEY_WIDTH:], QUERY_COLS))
        p_c = h_c @ (w_in[l] if need_ctx else w_in[l][:, :KEY_WIDTH])
        ctxd = split_cols(p_c[..., :KEY_WIDTH], KEY_COLS)
        if need_ctx:
            ctxd.update(split_cols(p_c[..., KEY_WIDTH:], QUERY_COLS))

        ya_l, ya_c = hgrn2_mixer(lat, ctxd, lb_all[l], hgrn_norm_g[l], need_ctx)
        yb_l, yb_c = mlstm_mixer(lat, ctxd, b_mlstm_gates[l], mlstm_conv_w[l], mlstm_norm_g[l], need_ctx)
        yc_l, yc_c = gqa_mixer(lat, ctxd, gqa_q_norm_g[l], gqa_k_norm_g[l], rope_c, need_ctx)
        yd_l, yd_c = mla_mixer(lat, ctxd, mla_q_norm_g[l], mla_kv_norm_g[l], w_mla_uq[l], w_mla_uk[l],
                               w_mla_uv[l], rope_d, need_ctx)

        x_l = x_l + mod[2] * (merge_branches((ya_l, yb_l, yc_l, yd_l), lat['gates'], w_branch[l]) @ w_out[l])
        x_l = x_l + mod[5] * squared_relu_mlp(modulate(rms_norm(x_l, g_norm2[l]), mod[3], mod[4]), w_ff1[l], w_ff2[l])

        if need_ctx:
            x_c = x_c + mod_c[2] * (merge_branches((ya_c, yb_c, yc_c, yd_c), ctxd['gates'], w_branch[l]) @ w_out[l])
            x_c = x_c + mod_c[5] * squared_relu_mlp(modulate(rms_norm(x_c, g_norm2[l]), mod_c[3], mod_c[4]),
                                                    w_ff1[l], w_ff2[l])
    return rms_norm(x_l, g_final)
```

```python
import functools

import numpy as np
import jax
import jax.numpy as jnp
from jax import lax
from jax.experimental import pallas as pl
from jax.experimental.pallas import tpu as pltpu

F32 = jnp.float32
BF16 = jnp.bfloat16

D_MODEL = 1024
GRID_W = 64
EPS = 1e-6
NEG = -1e30
ROPE_THETA = 10000.0
CHUNK = 64
SUB = 16
LANES = 128

A_HEADS, A_DK, A_DV = 4, 128, 128
B_HEADS, B_DQK, B_DV, B_CONV = 4, 64, 128, 3
C_HEADS, C_KV_HEADS, C_DH = 8, 2, 64
D_HEADS, D_Q_LORA, D_KV_LORA, D_NOPE, D_ROPE, D_DV = 4, 256, 128, 128, 64, 128
N_BRANCH, BRANCH_W = 4, 512
D_FF = 4 * D_MODEL

_REF_COLS = (
    ('a_i', 512), ('a_f_fwd', 512), ('a_f_bwd', 512), ('b_k', 256), ('b_v', 512), ('b_gates', 16),
    ('c_k', 128), ('c_v', 128), ('d_ckv', 128), ('d_krope', 64),
    ('a_q', 512), ('a_g', 512), ('b_q', 256), ('b_o', 512), ('c_q', 512), ('d_cq', 256), ('gates', 4096),
)
_REF_OFF = {}
_o = 0
for _n, _w in _REF_COLS:
    _REF_OFF[_n] = (_o, _w)
    _o += _w

BLK = dict(gates=0, a_i=32, a_ff=36, a_fb=40, a_q=44, a_g=48, b_v=52, b_o=56, c_q=60, b_k=64, b_q=66,
           d_cq=68, c_k=70, c_v=71, d_ckv=72, misc0=73, misc1=74)
P_BLOCKS = 75
P_WIDTH = P_BLOCKS * LANES
GATE_LANE0 = 64

VMEM_LIMIT = 56 * 1024 * 1024


def _cp(sem, **kw):
    return pltpu.CompilerParams(dimension_semantics=sem, vmem_limit_bytes=VMEM_LIMIT, **kw)


def _pick(n, cands):
    for c in cands:
        if n % c == 0:
            return c
    raise ValueError(f'no tile for {n}')


def _silu(x):
    return x * jax.nn.sigmoid(x)


def _rms(xf, g):
    ms = jnp.mean(xf * xf, axis=-1, keepdims=True)
    return xf * lax.rsqrt(ms + EPS) * g


def _log_sigmoid(x):
    return jnp.minimum(x, 0.0) - jnp.log1p(jnp.exp(-jnp.abs(x)))


def _logaddexp(a, b):
    return jnp.maximum(a, b) + jnp.log1p(jnp.exp(-jnp.abs(a - b)))


def _dot(a, b):
    return jnp.dot(a, b, preferred_element_type=F32)


def _dot_nt(a, b):
    return lax.dot_general(a, b, (((1,), (1,)), ((), ())), preferred_element_type=F32)


def _dot_tn(a, b):
    return lax.dot_general(a, b, (((0,), (0,)), ((), ())), preferred_element_type=F32)


def _dot_exact(a, b):
    return jnp.dot(a, b, preferred_element_type=F32, precision=lax.Precision.HIGHEST)


def _mod_norm(x_ref, g_ref, modb_ref, modc_ref, row0, tm, n_ctx, k_shift, k_scale):
    y = _rms(x_ref[0], g_ref[...])
    row = row0 + lax.broadcasted_iota(jnp.int32, (tm, 1), 0)
    is_ctx = row < n_ctx
    shift = jnp.where(is_ctx, modc_ref[k_shift:k_shift + 1, :], modb_ref[0, k_shift:k_shift + 1, :])
    scale = jnp.where(is_ctx, modc_ref[k_scale:k_scale + 1, :], modb_ref[0, k_scale:k_scale + 1, :])
    return y * (1.0 + scale) + shift


def _res_gate(modb_ref, modc_ref, row0, tm, n_ctx, k):
    row = row0 + lax.broadcasted_iota(jnp.int32, (tm, 1), 0)
    return jnp.where(row < n_ctx, modc_ref[k:k + 1, :], modb_ref[0, k:k + 1, :])


def _ada_kernel(s_ref, w_ref, b_ref, o_ref):
    s = _silu(s_ref[...])
    o_ref[0] = _dot(s.astype(BF16), w_ref[0]) + b_ref[0]


def _ada_call(s_rows, w_ada, b_ada):
    depth, d, d6 = w_ada.shape
    r = s_rows.shape[0]
    return pl.pallas_call(
        _ada_kernel,
        grid=(depth, d6 // d),
        in_specs=[pl.BlockSpec((r, d), lambda l, j: (0, 0)),
                  pl.BlockSpec((1, d, d), lambda l, j: (l, 0, j)),
                  pl.BlockSpec((1, 1, d), lambda l, j: (l, 0, j))],
        out_specs=pl.BlockSpec((1, r, d), lambda l, j: (l, 0, j)),
        out_shape=jax.ShapeDtypeStruct((depth, r, d6), F32),
        compiler_params=_cp(('parallel', 'parallel')),
        name='ada',
    )(s_rows, w_ada, b_ada.reshape(depth, 1, d6))


def _inproj_kernel(x_ref, modb_ref, modc_ref, g_ref, w_ref, o_ref, h_ref, *, tm, n_ctx):
    @pl.when(pl.program_id(2) == 0)
    def _():
        h = _mod_norm(x_ref, g_ref, modb_ref, modc_ref, pl.program_id(1) * tm, tm, n_ctx, 0, 1)
        h_ref[...] = h.astype(BF16)

    o_ref[0] = _dot(h_ref[...], w_ref[...])


def _inproj_call(x, modb, modc, g, w, n_ctx):
    b, t, d = x.shape
    tm = _pick(t, (768, 512, 384, 256, 128))
    tn = _pick(P_WIDTH, (1920, 1280, 640))
    return pl.pallas_call(
        functools.partial(_inproj_kernel, tm=tm, n_ctx=n_ctx),
        grid=(b, t // tm, P_WIDTH // tn),
        in_specs=[pl.BlockSpec((1, tm, d), lambda bi, i, j: (bi, i, 0)),
                  pl.BlockSpec((1, 6, d), lambda bi, i, j: (bi, 0, 0)),
                  pl.BlockSpec((6, d), lambda bi, i, j: (0, 0)),
                  pl.BlockSpec((1, d), lambda bi, i, j: (0, 0)),
                  pl.BlockSpec((d, tn), lambda bi, i, j: (0, j))],
        out_specs=pl.BlockSpec((1, tm, tn), lambda bi, i, j: (bi, i, j)),
        out_shape=jax.ShapeDtypeStruct((b, t, P_WIDTH), F32),
        scratch_shapes=[pltpu.VMEM((tm, d), BF16)],
        compiler_params=_cp(('parallel', 'parallel', 'arbitrary')),
        name='inproj',
    )(x, modb, modc, g, w)


def _chunk_of_step(step, n_chunks, n_ctx_chunks, rev):
    if not rev:
        return step
    return jnp.where(step < n_ctx_chunks, n_ctx_chunks - 1 - step, n_chunks + n_ctx_chunks - 1 - step)


def _order_mask(n, rev):
    ti = lax.broadcasted_iota(jnp.int32, (n, n), 0)
    si = lax.broadcasted_iota(jnp.int32, (n, n), 1)
    return (si >= ti) if rev else (si <= ti)


def _hgrn_chunk(q, kk, v, lf, st, rev):
    n = CHUNK
    nb = n // SUB
    tri = _order_mask(n, rev).astype(F32)
    bcum = _dot_exact(tri, lf)
    last = 0 if rev else n - 1
    blast = bcum[last:last + 1, :]

    o_inter = _dot_nt((q * jnp.exp(bcum)).astype(BF16), st.astype(BF16))
    kdec = kk * jnp.exp(blast - bcum)
    st_new = jnp.exp(blast) * st + _dot_tn(v.astype(BF16), kdec.astype(BF16))

    ones = jnp.ones((A_DK, LANES), BF16)
    t_idx = lax.broadcasted_iota(jnp.int32, (SUB, 1), 0)
    outs = []
    for i in range(nb):
        lo = i * SUB
        bi, qi, ki, vi = bcum[lo:lo + SUB], q[lo:lo + SUB], kk[lo:lo + SUB], v[lo:lo + SUB]
        prods = []
        for s in range(SUB):
            valid = (t_idx <= s) if rev else (t_idx >= s)
            e = jnp.exp(jnp.where(valid, bi - bi[s:s + 1], NEG))
            prods.append(qi * ki[s:s + 1] * e)
        p = jnp.concatenate(prods, axis=0).astype(BF16)
        a_rows = _dot(p, ones)
        o_i = a_rows[0:SUB] * vi[0:1]
        for s in range(1, SUB):
            o_i = o_i + a_rows[s * SUB:(s + 1) * SUB] * vi[s:s + 1]
        if rev and i < nb - 1:
            ref = bcum[lo + SUB:lo + SUB + 1]
            prev = slice(lo + SUB, n)
        elif (not rev) and i > 0:
            ref = bcum[lo - 1:lo]
            prev = slice(0, lo)
        else:
            prev = None
        if prev is not None:
            qt = (qi * jnp.exp(bi - ref)).astype(BF16)
            kt = (kk[prev] * jnp.exp(ref - bcum[prev])).astype(BF16)
            a = _dot_nt(qt, kt)
            o_i = o_i + _dot(a.astype(BF16), v[prev].astype(BF16))
        outs.append(o_i)
    return o_inter + jnp.concatenate(outs, axis=0), st_new


def _hgrn_kernel(v_ref, ff_ref, fb_ref, q_ref, g_ref, llb_ref, l1m_ref, ng_ref, o_ref, acc_ref, *,
                 n_chunks, n_ctx_chunks):
    for d, f_ref in enumerate((ff_ref, fb_ref)):
        rev = d == 1
        llb = llb_ref[d:d + 1, :]
        l1m = l1m_ref[d:d + 1, :]

        def body(step, st, f_ref=f_ref, rev=rev, llb=llb, l1m=l1m, first=(d == 0)):
            c = _chunk_of_step(step, n_chunks, n_ctx_chunks, rev)
            rows = pl.ds(pl.multiple_of(c * CHUNK, CHUNK), CHUNK)
            fpre = f_ref[0, rows, :]
            softplus = jnp.log1p(jnp.exp(-jnp.abs(fpre)))
            lf = _logaddexp(llb, l1m + (jnp.minimum(fpre, 0.0) - softplus))
            kk = jnp.exp(l1m + (jnp.minimum(-fpre, 0.0) - softplus))
            q = _silu(q_ref[0, rows, :])
            o, st_new = _hgrn_chunk(q, kk, v_ref[0, rows, :], lf, st, rev)
            if first:
                acc_ref[rows, :] = o
            else:
                acc_ref[rows, :] = acc_ref[rows, :] + o
            return st_new

        lax.fori_loop(0, n_chunks, body, jnp.zeros((A_DV, A_DK), F32))

    o_ref[0] = _rms(acc_ref[...], ng_ref[...]) * _silu(g_ref[0])


def _hgrn_call(p, llb, l1m, ng, n_ctx):
    b, t, _ = p.shape
    col = lambda base: pl.BlockSpec((1, t, LANES), lambda bi, h: (bi, 0, base + h))
    par = pl.BlockSpec((2, LANES), lambda bi, h: (0, h))
    return pl.pallas_call(
        functools.partial(_hgrn_kernel, n_chunks=t // CHUNK, n_ctx_chunks=n_ctx // CHUNK),
        grid=(b, A_HEADS),
        in_specs=[col(BLK['a_i']), col(BLK['a_ff']), col(BLK['a_fb']), col(BLK['a_q']), col(BLK['a_g']),
                  par, par, pl.BlockSpec((1, LANES), lambda bi, h: (0, 0))],
        out_specs=pl.BlockSpec((1, t, LANES), lambda bi, h: (bi, 0, h)),
        out_shape=jax.ShapeDtypeStruct((b, t, A_HEADS * A_DV), F32),
        scratch_shapes=[pltpu.VMEM((t, A_DV), F32)],
        compiler_params=_cp(('parallel', 'parallel')),
        name='hgrn2',
    )(p, p, p, p, p, llb, l1m, ng)


def _mlstm_chunk(q, k, vaug, g, gt_bc, lane_i, lane_f, cmat, m, rev):
    n = CHUNK
    gcol, bcum = g
    tri = _order_mask(n, rev)
    last = 0 if rev else n - 1
    b_col = bcum[:, lane_f:lane_f + 1]
    ig_col = gcol[:, lane_i:lane_i + 1]
    b_row = gt_bc[lane_f:lane_f + 1, n:2 * n]
    ig_row = gt_bc[lane_i:lane_i + 1, 0:n]

    dmat = jnp.where(tri, b_col - b_row + ig_row, NEG)
    inter = b_col + m
    m_t = jnp.maximum(inter, jnp.max(dmat, axis=-1, keepdims=True))
    w_inter = jnp.exp(inter - m_t)
    qb = q.astype(BF16)
    s = _dot_nt(qb, k.astype(BF16)) * jnp.exp(dmat - m_t)
    vb = vaug.astype(BF16)
    num_aug = w_inter * _dot(qb, cmat.astype(BF16)) + _dot(s.astype(BF16), vb)
    num = num_aug[:, 0:B_DV]
    den = num_aug[:, B_DV:B_DV + 1]
    h = num / jnp.maximum(jnp.abs(den), jnp.exp(-m_t))

    m_new = m_t[last:last + 1, :]
    b_last = b_col[last:last + 1, :]
    dec = jnp.exp(b_last + m - m_new)
    wk = jnp.exp(b_last - b_col + ig_col - m_new)
    c_new = dec * cmat + _dot_tn((wk * k).astype(BF16), vb)
    return h, c_new, m_new


def _mlstm_kernel(kp_ref, qp_ref, v_ref, og_ref, misc_ref, cw_ref, gb_ref, ng_ref, o_ref,
                  qs_ref, ks_ref, acc_ref, *, t, n_ctx):
    n_chunks, n_ctx_chunks = t // CHUNK, n_ctx // CHUNK
    row = lax.broadcasted_iota(jnp.int32, (t, 1), 0)
    no_prev = (row == 0) | (row == n_ctx)
    no_next = (row == n_ctx - 1) | (row == t - 1)

    def conv_silu(x, w):
        xm = jnp.where(no_prev, 0.0, pltpu.roll(x, 1, 0))
        xp = jnp.where(no_next, 0.0, pltpu.roll(x, t - 1, 0))
        return _silu(w[0:1, :] * xm + w[1:2, :] * x + w[2:3, :] * xp)

    qs_ref[...] = conv_silu(qp_ref[0], cw_ref[0])
    ks_ref[...] = conv_silu(kp_ref[0], cw_ref[1]) * (B_DQK ** -0.5)

    lane = lax.broadcasted_iota(jnp.int32, (CHUNK, B_DV), 1)
    one_col = jnp.where(lane == 0, 1.0, 0.0).astype(F32)

    for d in range(2):
        rev = d == 1
        tri_f = _order_mask(CHUNK, rev).astype(F32)

        def body(step, carry, rev=rev, tri_f=tri_f, d=d):
            c = _chunk_of_step(step, n_chunks, n_ctx_chunks, rev)
            rows = pl.ds(pl.multiple_of(c * CHUNK, CHUNK), CHUNK)
            g = misc_ref[0, rows, :] + gb_ref[0]
            bcum = _dot_exact(tri_f, _log_sigmoid(g))
            gt_bc = jnp.concatenate([g, bcum], axis=0).T
            new = []
            for j in range(2):
                cmat, m = carry[2 * j], carry[2 * j + 1]
                lane_i = GATE_LANE0 + (2 * d) * 2 + j
                lane_f = GATE_LANE0 + (2 * d + 1) * 2 + j
                q = qs_ref[rows, j * B_DQK:(j + 1) * B_DQK]
                k = ks_ref[rows, j * B_DQK:(j + 1) * B_DQK]
                vaug = jnp.concatenate([v_ref[0, rows, j * B_DV:(j + 1) * B_DV], one_col], axis=1)
                h, c_new, m_new = _mlstm_chunk(q, k, vaug, (g, bcum), gt_bc, lane_i, lane_f, cmat, m, rev)
                cols = slice(j * B_DV, (j + 1) * B_DV)
                if d == 0:
                    acc_ref[rows, cols] = h
                else:
                    acc_ref[rows, cols] = acc_ref[rows, cols] + h
                new += [c_new, m_new]
            return tuple(new)

        zero = (jnp.zeros((B_DQK, 2 * B_DV), F32), jnp.full((1, 1), NEG, F32))
        lax.fori_loop(0, n_chunks, body, zero + zero)

    for j in range(2):
        cols = slice(j * B_DV, (j + 1) * B_DV)
        o_ref[0, :, cols] = _rms(acc_ref[:, cols], ng_ref[...]) * jax.nn.sigmoid(og_ref[0, :, cols])


def _mlstm_call(p, conv_w, gate_bias, ng, n_ctx):
    b, t, _ = p.shape
    pairs = B_HEADS // 2
    return pl.pallas_call(
        functools.partial(_mlstm_kernel, t=t, n_ctx=n_ctx),
        grid=(b, pairs),
        in_specs=[pl.BlockSpec((1, t, LANES), lambda bi, hp: (bi, 0, BLK['b_k'] + hp)),
                  pl.BlockSpec((1, t, LANES), lambda bi, hp: (bi, 0, BLK['b_q'] + hp)),
                  pl.BlockSpec((1, t, 2 * B_DV), lambda bi, hp: (bi, 0, BLK['b_v'] // 2 + hp)),
                  pl.BlockSpec((1, t, 2 * B_DV), lambda bi, hp: (bi, 0, BLK['b_o'] // 2 + hp)),
                  pl.BlockSpec((1, t, LANES), lambda bi, hp: (bi, 0, BLK['misc0'] + hp)),
                  pl.BlockSpec((2, B_CONV, LANES), lambda bi, hp: (0, 0, hp)),
                  pl.BlockSpec((1, 1, LANES), lambda bi, hp: (hp, 0, 0)),
                  pl.BlockSpec((1, LANES), lambda bi, hp: (0, 0))],
        out_specs=pl.BlockSpec((1, t, 2 * B_DV), lambda bi, hp: (bi, 0, hp)),
        out_shape=jax.ShapeDtypeStruct((b, t, B_HEADS * B_DV), F32),
        scratch_shapes=[pltpu.VMEM((t, LANES), F32), pltpu.VMEM((t, LANES), F32),
                        pltpu.VMEM((t, 2 * B_DV), F32)],
        compiler_params=_cp(('parallel', 'parallel')),
        name='mlstm',
    )(p, p, p, p, p, conv_w, gate_bias, ng)


def _rope_tile(x, cos, sin_signed):
    lane = lax.broadcasted_iota(jnp.int32, (1, LANES), 1)
    first = (lane // (C_DH // 4)) % 2 == 0
    partner = jnp.where(first, pltpu.roll(x, LANES - C_DH // 4, 1), pltpu.roll(x, C_DH // 4, 1))
    return x * cos + partner * sin_signed


def _head_norm_tile(x, g, bd):
    sq = x * x
    hi = sq.astype(BF16)
    lo = (sq - hi.astype(F32)).astype(BF16)
    ms = _dot(hi, bd) + _dot(lo, bd)
    return x * lax.rsqrt(ms + EPS) * g


def _half_to(x, src_hi, dst_hi):
    lane = lax.broadcasted_iota(jnp.int32, (1, LANES), 1)
    if src_hi != dst_hi:
        x = pltpu.roll(x, LANES // 2, 1)
    keep = (lane >= LANES // 2) if dst_hi else (lane < LANES // 2)
    return jnp.where(keep, x, 0.0)


def _gqa_prep_kernel(cq_ref, ck_ref, cv_ref, cos_ref, sin_ref, qg_ref, kg_ref, bd_ref,
                     q_ref, k_ref, v_ref):
    cos, sin, bd = cos_ref[...], sin_ref[...], bd_ref[...]
    k = _head_norm_tile(ck_ref[0], kg_ref[...], bd)
    k_ref[0] = _rope_tile(k, cos, sin).astype(BF16)
    v_ref[0] = cv_ref[0].astype(BF16)
    grp = C_HEADS // C_KV_HEADS
    for mt in range(C_HEADS // 2):
        x = _head_norm_tile(cq_ref[0, :, mt * LANES:(mt + 1) * LANES], qg_ref[...], bd)
        x = _rope_tile(x, cos, sin)
        for half in range(2):
            head = 2 * mt + half
            q_ref[0, head] = _half_to(x, half == 1, head // grp == 1).astype(BF16)


def _gqa_prep_call(p, cos, sin, qg, kg, bd):
    b, t, _ = p.shape
    tm = _pick(t, (256, 128))
    tok = lambda blk, w: pl.BlockSpec((1, tm, w), lambda bi, i: (bi, i, blk))
    tab = pl.BlockSpec((tm, LANES), lambda bi, i: (i, 0))
    vec = pl.BlockSpec((1, LANES), lambda bi, i: (0, 0))
    kv_spec = pl.BlockSpec((1, tm, LANES), lambda bi, i: (bi, i, 0))
    return pl.pallas_call(
        _gqa_prep_kernel,
        grid=(b, t // tm),
        in_specs=[tok(BLK['c_q'] // 4, 4 * LANES), tok(BLK['c_k'], LANES), tok(BLK['c_v'], LANES),
                  tab, tab, vec, vec, pl.BlockSpec((LANES, LANES), lambda bi, i: (0, 0))],
        out_specs=[pl.BlockSpec((1, C_HEADS, tm, LANES), lambda bi, i: (bi, 0, i, 0)), kv_spec, kv_spec],
        out_shape=[jax.ShapeDtypeStruct((b, C_HEADS, t, LANES), BF16),
                   jax.ShapeDtypeStruct((b, t, LANES), BF16),
                   jax.ShapeDtypeStruct((b, t, LANES), BF16)],
        compiler_params=_cp(('parallel', 'parallel')),
        name='gqa_prep',
    )(p, p, p, cos, sin, qg, kg, bd)


def _softmax_pv(q, k, v, scale):
    s = _dot_nt(q, k) * scale
    m = jnp.max(s, axis=-1, keepdims=True)
    e = jnp.exp(s - m)
    l = jnp.sum(e, axis=-1, keepdims=True)
    return _dot(e.astype(BF16), v) / l


def _gqa_attn_kernel(q_ref, k_ref, v_ref, o_ref, *, tq, t, n_ctx):
    grp = C_HEADS // C_KV_HEADS
    lane = lax.broadcasted_iota(jnp.int32, (1, LANES), 1)

    def attend(nk):
        k, v = k_ref[0, 0:nk, :], v_ref[0, 0:nk, :]
        for g in range(C_KV_HEADS):
            q = q_ref[0, g * grp:(g + 1) * grp].reshape(grp * tq, LANES)
            o = _softmax_pv(q, k, v, C_DH ** -0.5)
            for pair in range(grp // 2):
                o_even = o[(2 * pair) * tq:(2 * pair + 1) * tq]
                o_odd = o[(2 * pair + 1) * tq:(2 * pair + 2) * tq]
                lo = o_even if g == 0 else pltpu.roll(o_even, LANES // 2, 1)
                hi = pltpu.roll(o_odd, LANES // 2, 1) if g == 0 else o_odd
                col = (g * grp // 2 + pair) * LANES
                o_ref[0, :, col:col + LANES] = jnp.where(lane < LANES // 2, lo, hi)

    is_ctx = pl.program_id(1) * tq < n_ctx
    pl.when(is_ctx)(lambda: attend(n_ctx))
    pl.when(jnp.logical_not(is_ctx))(lambda: attend(t))


def _gqa_attn_call(q, k, v, n_ctx):
    b, _, t, _ = q.shape
    tq = _pick(n_ctx, (128,))
    kv_spec = pl.BlockSpec((1, t, LANES), lambda bi, i: (bi, 0, 0))
    return pl.pallas_call(
        functools.partial(_gqa_attn_kernel, tq=tq, t=t, n_ctx=n_ctx),
        grid=(b, t // tq),
        in_specs=[pl.BlockSpec((1, C_HEADS, tq, LANES), lambda bi, i: (bi, 0, i, 0)), kv_spec, kv_spec],
        out_specs=pl.BlockSpec((1, tq, C_HEADS * C_DH), lambda bi, i: (bi, i, 0)),
        out_shape=jax.ShapeDtypeStruct((b, t, C_HEADS * C_DH), F32),
        compiler_params=_cp(('parallel', 'parallel')),
        name='gqa_attn',
    )(q, k, v)


def _mla_prep_kernel(ckv_ref, misc_ref, cq_ref, cos_ref, sin_ref, kvg_ref, qg_ref,
                     wuk_ref, wuv_ref, wqn_ref, wqr_ref, q_ref, k_ref, v_ref):
    cos, sin = cos_ref[...], sin_ref[...]
    ckv = _rms(ckv_ref[0], kvg_ref[...]).astype(BF16)
    k_nope = _dot(ckv, wuk_ref[...])
    v_ref[0] = _dot(ckv, wuv_ref[...]).astype(BF16)
    k_rope = _half_to(_rope_tile(misc_ref[0], cos, sin), False, False).astype(BF16)
    cq = _rms(cq_ref[0], qg_ref[...]).astype(BF16)
    q_nope = _dot(cq, wqn_ref[...])
    q_rope = _dot(cq, wqr_ref[...])
    for mt in range(D_HEADS // 2):
        qr = _rope_tile(q_rope[:, mt * LANES:(mt + 1) * LANES], cos, sin)
        for half in range(2):
            h = 2 * mt + half
            q_ref[0, h, :, 0:D_NOPE] = q_nope[:, h * D_NOPE:(h + 1) * D_NOPE].astype(BF16)
            q_ref[0, h, :, D_NOPE:2 * D_NOPE] = _half_to(qr, half == 1, False).astype(BF16)
            k_ref[0, h, :, 0:D_NOPE] = k_nope[:, h * D_NOPE:(h + 1) * D_NOPE].astype(BF16)
            k_ref[0, h, :, D_NOPE:2 * D_NOPE] = k_rope


def _mla_prep_call(p, cos, sin, kvg, qg, wuk, wuv, wqn, wqr):
    b, t, _ = p.shape
    tm = _pick(t, (256, 128))
    tok = lambda blk, w: pl.BlockSpec((1, tm, w), lambda bi, i: (bi, i, blk))
    tab = pl.BlockSpec((tm, LANES), lambda bi, i: (i, 0))
    full = lambda a: pl.BlockSpec(a.shape, lambda bi, i: (0,) * a.ndim)
    hd = 2 * D_NOPE
    qk_spec = pl.BlockSpec((1, D_HEADS, tm, hd), lambda bi, i: (bi, 0, i, 0))
    return pl.pallas_call(
        _mla_prep_kernel,
        grid=(b, t // tm),
        in_specs=[tok(BLK['d_ckv'], LANES), tok(BLK['misc0'], LANES), tok(BLK['d_cq'] // 2, 2 * LANES),
                  tab, tab, full(kvg), full(qg), full(wuk), full(wuv), full(wqn), full(wqr)],
        out_specs=[qk_spec, qk_spec, pl.BlockSpec((1, tm, D_HEADS * D_DV), lambda bi, i: (bi, i, 0))],
        out_shape=[jax.ShapeDtypeStruct((b, D_HEADS, t, hd), BF16),
                   jax.ShapeDtypeStruct((b, D_HEADS, t, hd), BF16),
                   jax.ShapeDtypeStruct((b, t, D_HEADS * D_DV), BF16)],
        compiler_params=_cp(('parallel', 'parallel')),
        name='mla_prep',
    )(p, p, p, cos, sin, kvg, qg, wuk, wuv, wqn, wqr)


def _mla_attn_kernel(q_ref, k_ref, v_ref, o_ref, *, tq, t, n_ctx):
    def attend(nk):
        o_ref[0] = _softmax_pv(q_ref[0, 0], k_ref[0, 0, 0:nk, :], v_ref[0, 0:nk, :],
                               (D_NOPE + D_ROPE) ** -0.5)

    is_ctx = pl.program_id(2) * tq < n_ctx
    pl.when(is_ctx)(lambda: attend(n_ctx))
    pl.when(jnp.logical_not(is_ctx))(lambda: attend(t))


def _mla_attn_call(q, k, v, n_ctx):
    b, _, t, hd = q.shape
    tq = _pick(n_ctx, (256, 128))
    return pl.pallas_call(
        functools.partial(_mla_attn_kernel, tq=tq, t=t, n_ctx=n_ctx),
        grid=(b, D_HEADS, t // tq),
        in_specs=[pl.BlockSpec((1, 1, tq, hd), lambda bi, h, i: (bi, h, i, 0)),
                  pl.BlockSpec((1, 1, t, hd), lambda bi, h, i: (bi, h, 0, 0)),
                  pl.BlockSpec((1, t, D_DV), lambda bi, h, i: (bi, 0, h))],
        out_specs=pl.BlockSpec((1, tq, D_DV), lambda bi, h, i: (bi, i, h)),
        out_shape=jax.ShapeDtypeStruct((b, t, D_HEADS * D_DV), F32),
        compiler_params=_cp(('parallel', 'parallel', 'parallel')),
        name='mla_attn',
    )(q, k, v)


def _merge_kernel(ya_ref, yb_ref, yc_ref, yd_ref, g0_ref, g1_ref, g2_ref, g3_ref, x_ref, modb_ref, modc_ref,
                  wb_ref, wo_ref, o_ref, *, tm, n_ctx):
    acc = None
    for r, (y_ref, g_ref) in enumerate(((ya_ref, g0_ref), (yb_ref, g1_ref), (yc_ref, g2_ref), (yd_ref, g3_ref))):
        term = jax.nn.sigmoid(g_ref[0]) * _dot(y_ref[0].astype(BF16), wb_ref[r])
        acc = term if acc is None else acc + term
    z = _dot(acc.astype(BF16), wo_ref[...])
    gate = _res_gate(modb_ref, modc_ref, pl.program_id(1) * tm, tm, n_ctx, 2)
    o_ref[0] = x_ref[0] + gate * z


def _merge_call(ys, p, x, modb, modc, wb, wo, n_ctx):
    b, t, d = x.shape
    tm = _pick(t, (256, 128))
    y_spec = pl.BlockSpec((1, tm, BRANCH_W), lambda bi, i: (bi, i, 0))
    g_specs = [pl.BlockSpec((1, tm, d), lambda bi, i, r=r: (bi, i, BLK['gates'] // 8 + r)) for r in range(N_BRANCH)]
    x_spec = pl.BlockSpec((1, tm, d), lambda bi, i: (bi, i, 0))
    return pl.pallas_call(
        functools.partial(_merge_kernel, tm=tm, n_ctx=n_ctx),
        grid=(b, t // tm),
        in_specs=[y_spec] * 4 + g_specs + [
            x_spec, pl.BlockSpec((1, 6, d), lambda bi, i: (bi, 0, 0)), pl.BlockSpec((6, d), lambda bi, i: (0, 0)),
            pl.BlockSpec((N_BRANCH, BRANCH_W, d), lambda bi, i: (0, 0, 0)),
            pl.BlockSpec((d, d), lambda bi, i: (0, 0))],
        out_specs=x_spec,
        out_shape=jax.ShapeDtypeStruct((b, t, d), F32),
        compiler_params=_cp(('parallel', 'parallel')),
        name='merge',
    )(*ys, p, p, p, p, x, modb, modc, wb, wo)


def _ffn_kernel(x_ref, modb_ref, modc_ref, g_ref, w1_ref, w2_ref, o_ref, h_ref, acc_ref, *, tm, n_ctx):
    j = pl.program_id(2)
    row0 = pl.program_id(1) * tm

    @pl.when(j == 0)
    def _():
        h_ref[...] = _mod_norm(x_ref, g_ref, modb_ref, modc_ref, row0, tm, n_ctx, 3, 4).astype(BF16)

    u = jnp.square(jnp.maximum(_dot(h_ref[...], w1_ref[...]), 0.0))
    part = _dot(u.astype(BF16), w2_ref[...])

    @pl.when(j == 0)
    def _():
        acc_ref[...] = part

    @pl.when(j > 0)
    def _():
        acc_ref[...] = acc_ref[...] + part

    @pl.when(j == pl.num_programs(2) - 1)
    def _():
        o_ref[0] = x_ref[0] + _res_gate(modb_ref, modc_ref, row0, tm, n_ctx, 5) * acc_ref[...]


def _ffn_call(x, modb, modc, g, w1, w2, n_ctx):
    b, t, d = x.shape
    tm = _pick(t, (768, 512, 384, 256, 128))
    tf = 1024
    x_spec = pl.BlockSpec((1, tm, d), lambda bi, i, j: (bi, i, 0))
    return pl.pallas_call(
        functools.partial(_ffn_kernel, tm=tm, n_ctx=n_ctx),
        grid=(b, t // tm, D_FF // tf),
        in_specs=[x_spec, pl.BlockSpec((1, 6, d), lambda bi, i, j: (bi, 0, 0)),
                  pl.BlockSpec((6, d), lambda bi, i, j: (0, 0)), pl.BlockSpec((1, d), lambda bi, i, j: (0, 0)),
                  pl.BlockSpec((d, tf), lambda bi, i, j: (0, j)), pl.BlockSpec((tf, d), lambda bi, i, j: (j, 0))],
        out_specs=x_spec,
        out_shape=jax.ShapeDtypeStruct((b, t, d), F32),
        scratch_shapes=[pltpu.VMEM((tm, d), BF16), pltpu.VMEM((tm, d), F32)],
        compiler_params=_cp(('parallel', 'parallel', 'arbitrary')),
        name='ffn',
    )(x, modb, modc, g, w1, w2)


def _final_kernel(x_ref, g_ref, o_ref):
    o_ref[0] = _rms(x_ref[0], g_ref[...])


def _final_call(x, g, n_ctx):
    b, t, d = x.shape
    tm = _pick(n_ctx, (256, 128))
    off = n_ctx // tm
    return pl.pallas_call(
        _final_kernel,
        grid=(b, (t - n_ctx) // tm),
        in_specs=[pl.BlockSpec((1, tm, d), lambda bi, i: (bi, i + off, 0)), pl.BlockSpec((1, d), lambda bi, i: (0, 0))],
        out_specs=pl.BlockSpec((1, tm, d), lambda bi, i: (bi, i, 0)),
        out_shape=jax.ShapeDtypeStruct((b, t - n_ctx, d), F32),
        compiler_params=_cp(('parallel', 'parallel')),
        name='final_norm',
    )(x, g)


def _pack_w_in(w_in):
    depth, d, _ = w_in.shape
    ref = lambda n: w_in[:, :, _REF_OFF[n][0]:_REF_OFF[n][0] + _REF_OFF[n][1]]
    zeros = lambda w: jnp.zeros((depth, d, w), w_in.dtype)
    gates = ref('b_gates').reshape(depth, d, 4, B_HEADS)
    pair_gates = [gates[:, :, :, 2 * hp:2 * hp + 2].reshape(depth, d, 8) for hp in range(2)]
    misc0 = jnp.concatenate([ref('d_krope'), pair_gates[0], zeros(LANES - 72)], axis=-1)
    misc1 = jnp.concatenate([zeros(64), pair_gates[1], zeros(LANES - 72)], axis=-1)
    parts = [ref('gates'), ref('a_i'), ref('a_f_fwd'), ref('a_f_bwd'), ref('a_q'), ref('a_g'), ref('b_v'),
             ref('b_o'), ref('c_q'), ref('b_k'), ref('b_q'), ref('d_cq'), ref('c_k'), ref('c_v'), ref('d_ckv'),
             misc0, misc1]
    packed = jnp.concatenate(parts, axis=-1)
    assert packed.shape[-1] == P_WIDTH
    return packed.astype(BF16)


def _rope_tables(n_lat, n_ctx):
    rows = n_lat // GRID_W
    row = jnp.repeat(jnp.arange(rows, dtype=jnp.int32), GRID_W)
    col = jnp.broadcast_to(jnp.arange(GRID_W, dtype=jnp.int32), (rows, GRID_W)).reshape(-1)
    quarter = C_DH // 4
    inv_freq = ROPE_THETA ** (-jnp.arange(quarter, dtype=F32) / quarter)
    ang_r = row.astype(F32)[:, None] * inv_freq
    ang_c = col.astype(F32)[:, None] * inv_freq
    cr, sr, cc, sc = jnp.cos(ang_r), jnp.sin(ang_r), jnp.cos(ang_c), jnp.sin(ang_c)
    cos = jnp.concatenate([cr, cr, cc, cc], axis=-1)
    sin = jnp.concatenate([-sr, sr, -sc, sc], axis=-1)
    cos = jnp.concatenate([jnp.ones((n_ctx, C_DH), F32), cos], axis=0)
    sin = jnp.concatenate([jnp.zeros((n_ctx, C_DH), F32), sin], axis=0)
    return jnp.tile(cos, (1, 2)), jnp.tile(sin, (1, 2))


def kernel(x, c, ctx, c_ctx, w_ada, b_ada, g_norm1, g_norm2, w_in, b_mlstm_gates, hgrn_lb_logits,
           hgrn_norm_g, mlstm_conv_w, mlstm_norm_g, gqa_q_norm_g, gqa_k_norm_g, mla_q_norm_g,
           mla_kv_norm_g, w_mla_uq, w_mla_uk, w_mla_uv, w_branch, w_out, w_ff1, w_ff2, g_final):
    bsz, n_lat, d = x.shape
    n_ctx = ctx.shape[1]
    depth = w_ada.shape[0]
    assert d == D_MODEL and n_lat % GRID_W == 0 and n_ctx % 128 == 0 and n_lat % 128 == 0

    w_in_p = _pack_w_in(w_in)
    w_ada_b = w_ada.astype(BF16)
    wb_b, wo_b = w_branch.astype(BF16), w_out.astype(BF16)
    w1_b, w2_b = w_ff1.astype(BF16), w_ff2.astype(BF16)
    wuk_b, wuv_b = w_mla_uk.astype(BF16), w_mla_uv.astype(BF16)
    uq = w_mla_uq.reshape(depth, D_Q_LORA, D_HEADS, D_NOPE + D_ROPE)
    wqn_b = uq[..., :D_NOPE].reshape(depth, D_Q_LORA, D_HEADS * D_NOPE).astype(BF16)
    wqr_b = uq[..., D_NOPE:].reshape(depth, D_Q_LORA, D_HEADS * D_ROPE).astype(BF16)
    cos_t, sin_t = _rope_tables(n_lat, n_ctx)
    half = lax.broadcasted_iota(jnp.int32, (LANES, LANES), 0) // C_DH == lax.broadcasted_iota(
        jnp.int32, (LANES, LANES), 1) // C_DH
    bd = jnp.where(half, 1.0 / C_DH, 0.0).astype(BF16)
    lb = jnp.cumsum(jax.nn.softmax(hgrn_lb_logits.astype(F32), axis=0), axis=0)
    lb = lb - lb[0]
    llb, l1m = jnp.log(lb), jnp.log1p(-lb)
    gb = b_mlstm_gates.reshape(depth, 4, B_HEADS)
    gate_bias = jnp.zeros((depth, 2, 1, LANES), F32)
    for hp in range(2):
        gate_bias = gate_bias.at[:, hp, 0, GATE_LANE0:GATE_LANE0 + 8].set(gb[:, :, 2 * hp:2 * hp + 2].reshape(depth, 8))
    tile2 = lambda g: jnp.tile(g, (1, 2))

    pad = (-(bsz + 1)) % 8
    s_rows = jnp.concatenate([c, c_ctx[None, :], jnp.zeros((pad, d), F32)], axis=0)
    mods = _ada_call(s_rows, w_ada_b, b_ada)

    xa = jnp.concatenate([ctx, x], axis=1)
    for l in range(depth):
        modb = mods[l, :bsz].reshape(bsz, 6, d)
        modc = mods[l, bsz].reshape(6, d)
        p = _inproj_call(xa, modb, modc, g_norm1[l][None, :], w_in_p[l], n_ctx)
        ya = _hgrn_call(p, llb[l], l1m[l], hgrn_norm_g[l][None, :], n_ctx)
        yb = _mlstm_call(p, mlstm_conv_w[l], gate_bias[l], mlstm_norm_g[l][None, :], n_ctx)
        qc, kc, vc = _gqa_prep_call(p, cos_t, sin_t, tile2(gqa_q_norm_g[l][None, :]),
                                    tile2(gqa_k_norm_g[l][None, :]), bd)
        yc = _gqa_attn_call(qc, kc, vc, n_ctx)
        qd, kd, vd = _mla_prep_call(p, cos_t, sin_t, mla_kv_norm_g[l][None, :], mla_q_norm_g[l][None, :],
                                    wuk_b[l], wuv_b[l], wqn_b[l], wqr_b[l])
        yd = _mla_attn_call(qd, kd, vd, n_ctx)
        xa = _merge_call((ya, yb, yc, yd), p, xa, modb, modc, wb_b[l], wo_b[l], n_ctx)
        xa = _ffn_call(xa, modb, modc, g_norm2[l][None, :], w1_b[l], w2_b[l], n_ctx)
    return _final_call(xa, g_final[None, :], n_ctx)
```

```python
import functools

import numpy as np
import jax
import jax.numpy as jnp
from jax import lax
from jax.experimental import pallas as pl
from jax.experimental.pallas import tpu as pltpu

F32 = jnp.float32
BF16 = jnp.bfloat16

D_MODEL = 1024
GRID_W = 64
EPS = 1e-6
NEG = -1e30
ROPE_THETA = 10000.0
CHUNK = 64
LANES = 128

A_HEADS, A_DK, A_DV = 4, 128, 128
B_HEADS, B_DQK, B_DV, B_CONV = 4, 64, 128, 3
C_HEADS, C_KV_HEADS, C_DH = 8, 2, 64
D_HEADS, D_Q_LORA, D_KV_LORA, D_NOPE, D_ROPE, D_DV = 4, 256, 128, 128, 64, 128
N_BRANCH, BRANCH_W = 4, 512
D_FF = 4 * D_MODEL

_REF_COLS = (
    ('a_i', 512), ('a_f_fwd', 512), ('a_f_bwd', 512), ('b_k', 256), ('b_v', 512), ('b_gates', 16),
    ('c_k', 128), ('c_v', 128), ('d_ckv', 128), ('d_krope', 64),
    ('a_q', 512), ('a_g', 512), ('b_q', 256), ('b_o', 512), ('c_q', 512), ('d_cq', 256), ('gates', 4096),
)
_REF_OFF = {}
_o = 0
for _n, _w in _REF_COLS:
    _REF_OFF[_n] = (_o, _w)
    _o += _w

BLK = dict(gates=0, a_i=32, a_ff=36, a_fb=40, a_q=44, a_g=48, b_v=52, b_o=56, c_q=60, b_k=64, b_q=66,
           d_cq=68, c_k=70, c_v=71, d_ckv=72, misc0=73, misc1=74)
P_BLOCKS = 75
P_WIDTH = P_BLOCKS * LANES
GATE_LANE0 = 64

VMEM_LIMIT = 56 * 1024 * 1024


def _cp(sem, **kw):
    return pltpu.CompilerParams(dimension_semantics=sem, vmem_limit_bytes=VMEM_LIMIT, **kw)


def _pick(n, cands):
    for c in cands:
        if n % c == 0:
            return c
    raise ValueError(f'no tile for {n}')


def _sigmoid(x):
    return 1.0 / (1.0 + jnp.exp(-x))


def _silu(x):
    return x * _sigmoid(x)


def _softplus_neg_abs(x):
    return jnp.log(1.0 + jnp.exp(-jnp.abs(x)))


def _rms(xf, g):
    ms = jnp.mean(xf * xf, axis=-1, keepdims=True)
    return xf * lax.rsqrt(ms + EPS) * g


def _log_sigmoid(x):
    return jnp.minimum(x, 0.0) - _softplus_neg_abs(x)


def _logaddexp(a, b):
    return jnp.maximum(a, b) + _softplus_neg_abs(a - b)


def _dot(a, b):
    return jnp.dot(a, b, preferred_element_type=F32)


def _dot_nt(a, b):
    return lax.dot_general(a, b, (((1,), (1,)), ((), ())), preferred_element_type=F32)


def _dot_tn(a, b):
    return lax.dot_general(a, b, (((0,), (0,)), ((), ())), preferred_element_type=F32)


def _mod_norm(x_ref, g_ref, modb_ref, modc_ref, row0, tm, n_ctx, k_shift, k_scale):
    y = _rms(x_ref[0], g_ref[...])
    row = row0 + lax.broadcasted_iota(jnp.int32, (tm, 1), 0)
    is_ctx = row < n_ctx
    shift = jnp.where(is_ctx, modc_ref[k_shift:k_shift + 1, :], modb_ref[0, k_shift:k_shift + 1, :])
    scale = jnp.where(is_ctx, modc_ref[k_scale:k_scale + 1, :], modb_ref[0, k_scale:k_scale + 1, :])
    return y * (1.0 + scale) + shift


def _res_gate(modb_ref, modc_ref, row0, tm, n_ctx, k):
    row = row0 + lax.broadcasted_iota(jnp.int32, (tm, 1), 0)
    return jnp.where(row < n_ctx, modc_ref[k:k + 1, :], modb_ref[0, k:k + 1, :])


def _ada_kernel(s_ref, w_ref, b_ref, o_ref):
    s = _silu(s_ref[...])
    o_ref[0] = _dot(s.astype(BF16), w_ref[0]) + b_ref[0]


def _ada_call(s_rows, w_ada, b_ada):
    depth, d, d6 = w_ada.shape
    r = s_rows.shape[0]
    return pl.pallas_call(
        _ada_kernel,
        grid=(depth, d6 // d),
        in_specs=[pl.BlockSpec((r, d), lambda l, j: (0, 0)),
                  pl.BlockSpec((1, d, d), lambda l, j: (l, 0, j)),
                  pl.BlockSpec((1, 1, d), lambda l, j: (l, 0, j))],
        out_specs=pl.BlockSpec((1, r, d), lambda l, j: (l, 0, j)),
        out_shape=jax.ShapeDtypeStruct((depth, r, d6), F32),
        compiler_params=_cp(('parallel', 'parallel')),
        name='ada',
    )(s_rows, w_ada, b_ada.reshape(depth, 1, d6))


def _inproj_kernel(x_ref, modb_ref, modc_ref, g_ref, w_ref, o_ref, h_ref, *, tm, n_ctx):
    @pl.when(pl.program_id(2) == 0)
    def _():
        h = _mod_norm(x_ref, g_ref, modb_ref, modc_ref, pl.program_id(1) * tm, tm, n_ctx, 0, 1)
        h_ref[...] = h.astype(BF16)

    o_ref[0] = _dot(h_ref[...], w_ref[...])


def _inproj_call(x, modb, modc, g, w, n_ctx):
    b, t, d = x.shape
    tm = _pick(t, (768, 512, 384, 256, 128))
    tn = _pick(P_WIDTH, (1920, 1280, 640))
    return pl.pallas_call(
        functools.partial(_inproj_kernel, tm=tm, n_ctx=n_ctx),
        grid=(b, t // tm, P_WIDTH // tn),
        in_specs=[pl.BlockSpec((1, tm, d), lambda bi, i, j: (bi, i, 0)),
                  pl.BlockSpec((1, 6, d), lambda bi, i, j: (bi, 0, 0)),
                  pl.BlockSpec((6, d), lambda bi, i, j: (0, 0)),
                  pl.BlockSpec((1, d), lambda bi, i, j: (0, 0)),
                  pl.BlockSpec((d, tn), lambda bi, i, j: (0, j))],
        out_specs=pl.BlockSpec((1, tm, tn), lambda bi, i, j: (bi, i, j)),
        out_shape=jax.ShapeDtypeStruct((b, t, P_WIDTH), F32),
        scratch_shapes=[pltpu.VMEM((tm, d), BF16)],
        compiler_params=_cp(('parallel', 'parallel', 'arbitrary')),
        name='inproj',
    )(x, modb, modc, g, w)


def _chunk_of_step(step, n_chunks, n_ctx_chunks, rev):
    if not rev:
        return step
    return jnp.where(step < n_ctx_chunks, n_ctx_chunks - 1 - step, n_chunks + n_ctx_chunks - 1 - step)


def _order_mask(n, rev):
    ti = lax.broadcasted_iota(jnp.int32, (n, n), 0)
    si = lax.broadcasted_iota(jnp.int32, (n, n), 1)
    return (si >= ti) if rev else (si <= ti)


_BLOCK_W = (2, 4, 8, 16, 32, 64)
_N_LEVELS = 7
A_HPS = 2


def _hgrn_tables():
    ti, si = np.meshgrid(np.arange(CHUNK), np.arange(CHUNK), indexing='ij')
    mats = [np.tile((ti // w == si // w) & (si <= ti), (1, 4)) for w in _BLOCK_W]
    lvl = np.full((2, CHUNK, CHUNK), -1, np.int32)
    for d in range(2):
        lvl[d][ti == si] = 0
        for i, w in enumerate((1, 2, 4, 8, 16, 32)):
            same = ti // (2 * w) == si // (2 * w)
            t_late = (ti // w) % 2 == (1 - d)
            s_early = (si // w) % 2 == d
            lvl[d][same & t_late & s_early] = i + 1
    return np.concatenate(mats, axis=0).astype(np.float32), lvl


def _block_total(p, w):
    n = p.shape[0]
    if w >= 8:
        parts = [jnp.broadcast_to(p[e - 1:e, :], (w, p.shape[1])) for e in range(w, n + 1, w)]
        return parts[0] if len(parts) == 1 else jnp.concatenate(parts, axis=0)
    x = p.reshape(n // 8, 8, p.shape[1])
    sub = lax.broadcasted_iota(jnp.int32, (1, 8, 1), 1)
    out = jnp.broadcast_to(x[:, 7:8, :], x.shape)
    for e in range(8 - w - 1, -1, -w):
        out = jnp.where(sub <= e, jnp.broadcast_to(x[:, e:e + 1, :], x.shape), out)
    return out.reshape(n, p.shape[1])


def _hgrn_kernel(v_ref, ff_ref, fb_ref, q_ref, g_ref, llb_ref, l1m_ref, ng_ref, bs_ref, lvl_ref, o_ref,
                 accf_ref, accb_ref, qs_ref, *, n_chunks, n_ctx_chunks):
    n = CHUNK
    nw = len(_BLOCK_W)
    f_refs, acc_refs = (ff_ref, fb_ref), (accf_ref, accb_ref)
    streams = [(hh, d) for hh in range(A_HPS) for d in range(2)]
    lanes = lambda hh: slice(hh * A_DK, (hh + 1) * A_DK)

    def body(step, carry):
        rows = []
        for d in range(2):
            c = _chunk_of_step(step, n_chunks, n_ctx_chunks, d == 1)
            rows.append(pl.ds(pl.multiple_of(c * n, n), n))
        lf, kk, q, v, sums = {}, {}, {}, {}, {}
        for hh, d in streams:
            fpre = f_refs[d][0, rows[d], lanes(hh)]
            l1m = l1m_ref[d:d + 1, lanes(hh)]
            softplus = _softplus_neg_abs(fpre)
            lf[hh, d] = _logaddexp(llb_ref[d:d + 1, lanes(hh)], l1m + (jnp.minimum(fpre, 0.0) - softplus))
            kk[hh, d] = jnp.exp(l1m + (jnp.minimum(-fpre, 0.0) - softplus))
            q[hh, d] = qs_ref[rows[d], lanes(hh)]
            v[hh, d] = v_ref[0, rows[d], lanes(hh)].astype(BF16)
        for hh in range(A_HPS):
            parts = []
            for d in range(2):
                hi = lf[hh, d].astype(BF16)
                lo = (lf[hh, d] - hi.astype(F32)).astype(BF16)
                zero = jnp.zeros_like(hi)
                parts += [jnp.concatenate([hi, zero] if d == 0 else [zero, hi], axis=1),
                          jnp.concatenate([lo, zero] if d == 0 else [zero, lo], axis=1)]
            sums_fb = _dot(bs_ref[...], jnp.concatenate(parts, axis=0))
            sums[hh, 0], sums[hh, 1] = sums_fb[:, 0:A_DK], sums_fb[:, A_DK:2 * A_DK]

        a_lv, o_inter, st_new = {}, {}, []
        for si, (hh, d) in enumerate(streams):
            key = (hh, d)
            pre = lambda i, key=key: sums[key][i * n:(i + 1) * n]
            rest = lambda i, pre=pre: _block_total(pre(i), _BLOCK_W[i]) - pre(i)
            if d == 0:
                eq, ek = pre, rest
            else:
                eq = lambda i, key=key, rest=rest: rest(i) + lf[key]
                ek = lambda i, key=key, pre=pre: pre(i) - lf[key]
            kb = kk[key].astype(BF16)
            q01 = jnp.concatenate([q[key], q[key] * jnp.exp(lf[key])], axis=0).astype(BF16)
            a01 = _dot_nt(q01, kb)
            mats = [a01[0:n], a01[n:2 * n]]
            for i in range(nw - 1):
                ql = (q[key] * jnp.exp(eq(i))).astype(BF16)
                kl = (kk[key] * jnp.exp(ek(i))).astype(BF16)
                mats.append(_dot_nt(ql, kl))
            a_lv[key] = mats
            st = carry[si]
            o_inter[key] = _dot_nt((q[key] * jnp.exp(eq(nw - 1))).astype(BF16), st.astype(BF16))
            kdec = (kk[key] * jnp.exp(ek(nw - 1))).astype(BF16)
            total = sums[key][(nw - 1) * n + n - 1:(nw - 1) * n + n]
            st_new.append(jnp.exp(total) * st + _dot_tn(v[key], kdec))

        o_intra = {}
        for key in streams:
            lvl = lvl_ref[key[1]]
            a = jnp.zeros((n, n), F32)
            for i in range(_N_LEVELS):
                a = jnp.where(lvl == i, a_lv[key][i], a)
            o_intra[key] = _dot(a.astype(BF16), v[key])
        for hh, d in streams:
            acc_refs[d][rows[d], lanes(hh)] = o_inter[hh, d] + o_intra[hh, d]
        return tuple(st_new)

    qs_ref[...] = _silu(q_ref[0])
    zero = jnp.zeros((A_DV, A_DK), F32)
    lax.fori_loop(0, n_chunks, body, (zero,) * len(streams))
    for hh in range(A_HPS):
        o = accf_ref[:, lanes(hh)] + accb_ref[:, lanes(hh)]
        o_ref[0, :, lanes(hh)] = _rms(o, ng_ref[...]) * _silu(g_ref[0, :, lanes(hh)])


def _hgrn_call(p, llb, l1m, ng, n_ctx):
    b, t, _ = p.shape
    bs_np, lvl_np = _hgrn_tables()
    bs, lvl = jnp.asarray(bs_np, BF16), jnp.asarray(lvl_np)
    wid = A_HPS * A_DK
    col = lambda base: pl.BlockSpec((1, t, wid), lambda bi, h: (bi, 0, base // A_HPS + h))
    par = pl.BlockSpec((2, wid), lambda bi, h: (0, h))
    return pl.pallas_call(
        functools.partial(_hgrn_kernel, n_chunks=t // CHUNK, n_ctx_chunks=n_ctx // CHUNK),
        grid=(b, A_HEADS // A_HPS),
        in_specs=[col(BLK['a_i']), col(BLK['a_ff']), col(BLK['a_fb']), col(BLK['a_q']), col(BLK['a_g']),
                  par, par, pl.BlockSpec((1, A_DV), lambda bi, h: (0, 0)),
                  pl.BlockSpec(bs.shape, lambda bi, h: (0, 0)), pl.BlockSpec(lvl.shape, lambda bi, h: (0, 0, 0))],
        out_specs=pl.BlockSpec((1, t, wid), lambda bi, h: (bi, 0, h)),
        out_shape=jax.ShapeDtypeStruct((b, t, A_HEADS * A_DV), F32),
        scratch_shapes=[pltpu.VMEM((t, wid), F32), pltpu.VMEM((t, wid), F32), pltpu.VMEM((t, wid), F32)],
        compiler_params=_cp(('parallel', 'parallel')),
        name='hgrn2',
    )(p, p, p, p, p, llb, l1m, ng, bs, lvl)


def _scan_max(x, rev):
    n = x.shape[0]
    row = lax.broadcasted_iota(jnp.int32, (n, 1), 0)
    k = 1
    while k < n:
        if rev:
            shifted = jnp.where(row < n - k, pltpu.roll(x, n - k, 0), NEG)
        else:
            shifted = jnp.where(row >= k, pltpu.roll(x, k, 0), NEG)
        x = jnp.maximum(x, shifted)
        k *= 2
    return x


def _lane_bcast(x, lane, width):
    return jnp.broadcast_to(x[:, lane:lane + 1], (x.shape[0], width))


def _mlstm_kernel(kp_ref, qp_ref, v_ref, og_ref, misc_ref, cw_ref, gb_ref, ng_ref, tri_ref, o_ref,
                  qs_ref, ks_ref, accf_ref, accb_ref, c_ref, *, t, n_ctx):
    n = CHUNK
    n_chunks, n_ctx_chunks = t // n, n_ctx // n
    row = lax.broadcasted_iota(jnp.int32, (t, 1), 0)
    no_prev = (row == 0) | (row == n_ctx)
    no_next = (row == n_ctx - 1) | (row == t - 1)

    def conv_silu(x, w):
        xm = jnp.where(no_prev, 0.0, pltpu.roll(x, 1, 0))
        xp = jnp.where(no_next, 0.0, pltpu.roll(x, t - 1, 0))
        return _silu(w[0:1, :] * xm + w[1:2, :] * x + w[2:3, :] * xp)

    qs_ref[...] = conv_silu(qp_ref[0], cw_ref[0])
    ks_ref[...] = conv_silu(kp_ref[0], cw_ref[1]) * (B_DQK ** -0.5)
    c_ref[...] = jnp.zeros(c_ref.shape, F32)

    lane = lax.broadcasted_iota(jnp.int32, (1, LANES), 1)
    head_lanes = (lane < B_DQK, lane >= B_DQK)
    row2 = lax.broadcasted_iota(jnp.int32, (2 * B_DQK, 1), 0)
    acc_refs = (accf_ref, accb_ref)
    streams = [(d, j) for d in range(2) for j in range(2)]
    gate_lane = lambda d, j: GATE_LANE0 + 4 * d + j

    def body(step, m_prev):
        rows, g_all, bal_all = [], [], []
        for d in range(2):
            c = _chunk_of_step(step, n_chunks, n_ctx_chunks, d == 1)
            rows.append(pl.ds(pl.multiple_of(c * n, n), n))
            g = misc_ref[0, rows[d], :] + gb_ref[0]
            ls = _log_sigmoid(g)
            hi = ls.astype(BF16)
            lo = (ls - hi.astype(F32)).astype(BF16)
            bc = _dot(tri_ref[d], hi) + _dot(tri_ref[d], lo)
            g_all.append(g)
            bal_all.append(pltpu.roll(bc, LANES - 2, 1))

        qm, kf, qk, qc = {}, {}, {}, {}
        for d, j in streams:
            kf[d] = ks_ref[rows[d], :]
            qm[d, j] = jnp.where(head_lanes[j], qs_ref[rows[d], :], 0.0).astype(BF16)
            qk[d, j] = _dot_nt(qm[d, j], kf[d].astype(BF16))
            qc[d, j] = _dot(qm[d, j], c_ref[d].astype(BF16))

        u_all, ut_all, cu_all, r_all, w_all, em_all, wk_all, dec_all, m_new = [], [], [], [], [], [], [], [], []
        for d in range(2):
            last = 0 if d == 1 else n - 1
            u = g_all[d] - bal_all[d]
            cu = _scan_max(u, d == 1)
            gm = jnp.maximum(m_prev[d], cu)
            w = jnp.exp(m_prev[d] - gm)
            gl = gm[last:last + 1]
            u_all.append(u)
            ut_all.append(jnp.concatenate([u, cu], axis=0).T[:, 0:n])
            cu_all.append(cu)
            r_all.append(jnp.exp(cu - gm))
            w_all.append(w)
            em_all.append(jnp.exp(-bal_all[d] - gm))
            wk_all.append(jnp.exp(u - gl))
            dec_all.append(w[last:last + 1])
            m_new.append(bal_all[d][last:last + 1] + gl)

        un, uk = {}, {}
        for d, j in streams:
            li = gate_lane(d, j)
            expo = ut_all[d][li:li + 1, :] - _lane_bcast(cu_all[d], li, n)
            s0 = (qk[d, j] * jnp.exp(jnp.where(_order_mask(n, d == 1), expo, NEG))).astype(BF16)
            one_col = jnp.broadcast_to(jnp.where(lane == li, 1.0, 0.0), (n, LANES))
            vaug = jnp.concatenate([v_ref[0, rows[d], j * B_DV:(j + 1) * B_DV], one_col], axis=1).astype(BF16)
            un[d, j] = _dot(s0, vaug)
            kw = jnp.where(head_lanes[j], kf[d], 0.0) * _lane_bcast(wk_all[d], li, LANES)
            uk[d, j] = _dot_tn(kw.astype(BF16), vaug)

        for d, j in streams:
            li = gate_lane(d, j)
            num = (_lane_bcast(w_all[d], li, B_DV) * qc[d, j][:, 0:B_DV]
                   + _lane_bcast(r_all[d], li, B_DV) * un[d, j][:, 0:B_DV])
            den = w_all[d] * qc[d, j][:, B_DV:] + r_all[d] * un[d, j][:, B_DV:]
            z = 1.0 / jnp.maximum(jnp.abs(den), em_all[d])
            acc_refs[d][rows[d], j * B_DV:(j + 1) * B_DV] = num * _lane_bcast(z, li, B_DV)
        for d in range(2):
            l0, l1 = gate_lane(d, 0), gate_lane(d, 1)
            dec = jnp.where(row2 < B_DQK, dec_all[d][:, l0:l0 + 1], dec_all[d][:, l1:l1 + 1])
            c_ref[d] = dec * c_ref[d] + uk[d, 0] + uk[d, 1]
        return tuple(m_new)

    m0 = jnp.full((1, LANES), NEG, F32)
    lax.fori_loop(0, n_chunks, body, (m0, m0))

    for j in range(2):
        cols = slice(j * B_DV, (j + 1) * B_DV)
        h = accf_ref[:, cols] + accb_ref[:, cols]
        o_ref[0, :, cols] = _rms(h, ng_ref[...]) * _sigmoid(og_ref[0, :, cols])


def _mlstm_call(p, conv_w, gate_bias, ng, n_ctx):
    b, t, _ = p.shape
    pairs = B_HEADS // 2
    tri = jnp.stack([_order_mask(CHUNK, False), _order_mask(CHUNK, True)]).astype(BF16)
    return pl.pallas_call(
        functools.partial(_mlstm_kernel, t=t, n_ctx=n_ctx),
        grid=(b, pairs),
        in_specs=[pl.BlockSpec((1, t, LANES), lambda bi, hp: (bi, 0, BLK['b_k'] + hp)),
                  pl.BlockSpec((1, t, LANES), lambda bi, hp: (bi, 0, BLK['b_q'] + hp)),
                  pl.BlockSpec((1, t, 2 * B_DV), lambda bi, hp: (bi, 0, BLK['b_v'] // 2 + hp)),
                  pl.BlockSpec((1, t, 2 * B_DV), lambda bi, hp: (bi, 0, BLK['b_o'] // 2 + hp)),
                  pl.BlockSpec((1, t, LANES), lambda bi, hp: (bi, 0, BLK['misc0'] + hp)),
                  pl.BlockSpec((2, B_CONV, LANES), lambda bi, hp: (0, 0, hp)),
                  pl.BlockSpec((1, 1, LANES), lambda bi, hp: (hp, 0, 0)),
                  pl.BlockSpec((1, LANES), lambda bi, hp: (0, 0)),
                  pl.BlockSpec((2, CHUNK, CHUNK), lambda bi, hp: (0, 0, 0))],
        out_specs=pl.BlockSpec((1, t, 2 * B_DV), lambda bi, hp: (bi, 0, hp)),
        out_shape=jax.ShapeDtypeStruct((b, t, B_HEADS * B_DV), F32),
        scratch_shapes=[pltpu.VMEM((t, LANES), F32), pltpu.VMEM((t, LANES), F32),
                        pltpu.VMEM((t, 2 * B_DV), F32), pltpu.VMEM((t, 2 * B_DV), F32),
                        pltpu.VMEM((2, 2 * B_DQK, 2 * B_DV), F32)],
        compiler_params=_cp(('parallel', 'parallel')),
        name='mlstm',
    )(p, p, p, p, p, conv_w, gate_bias, ng, tri)


def _rope_tile(x, cos, sin_signed):
    lane = lax.broadcasted_iota(jnp.int32, (1, LANES), 1)
    first = (lane // (C_DH // 4)) % 2 == 0
    partner = jnp.where(first, pltpu.roll(x, LANES - C_DH // 4, 1), pltpu.roll(x, C_DH // 4, 1))
    return x * cos + partner * sin_signed


def _head_norm_tile(x, g, bd):
    sq = x * x
    hi = sq.astype(BF16)
    lo = (sq - hi.astype(F32)).astype(BF16)
    ms = _dot(hi, bd) + _dot(lo, bd)
    return x * lax.rsqrt(ms + EPS) * g


def _half_to(x, src_hi, dst_hi):
    lane = lax.broadcasted_iota(jnp.int32, (1, LANES), 1)
    if src_hi != dst_hi:
        x = pltpu.roll(x, LANES // 2, 1)
    keep = (lane >= LANES // 2) if dst_hi else (lane < LANES // 2)
    return jnp.where(keep, x, 0.0)


def _gqa_prep_kernel(cq_ref, ck_ref, cv_ref, cos_ref, sin_ref, qg_ref, kg_ref, bd_ref,
                     q_ref, k_ref, v_ref):
    cos, sin, bd = cos_ref[...], sin_ref[...], bd_ref[...]
    k = _head_norm_tile(ck_ref[0], kg_ref[...], bd)
    k_ref[0] = _rope_tile(k, cos, sin).astype(BF16)
    v_ref[0] = cv_ref[0].astype(BF16)
    grp = C_HEADS // C_KV_HEADS
    for mt in range(C_HEADS // 2):
        x = _head_norm_tile(cq_ref[0, :, mt * LANES:(mt + 1) * LANES], qg_ref[...], bd)
        x = _rope_tile(x, cos, sin)
        for half in range(2):
            head = 2 * mt + half
            q_ref[0, head] = _half_to(x, half == 1, head // grp == 1).astype(BF16)


def _gqa_prep_call(p, cos, sin, qg, kg, bd):
    b, t, _ = p.shape
    tm = _pick(t, (256, 128))
    tok = lambda blk, w: pl.BlockSpec((1, tm, w), lambda bi, i: (bi, i, blk))
    tab = pl.BlockSpec((tm, LANES), lambda bi, i: (i, 0))
    vec = pl.BlockSpec((1, LANES), lambda bi, i: (0, 0))
    kv_spec = pl.BlockSpec((1, tm, LANES), lambda bi, i: (bi, i, 0))
    return pl.pallas_call(
        _gqa_prep_kernel,
        grid=(b, t // tm),
        in_specs=[tok(BLK['c_q'] // 4, 4 * LANES), tok(BLK['c_k'], LANES), tok(BLK['c_v'], LANES),
                  tab, tab, vec, vec, pl.BlockSpec((LANES, LANES), lambda bi, i: (0, 0))],
        out_specs=[pl.BlockSpec((1, C_HEADS, tm, LANES), lambda bi, i: (bi, 0, i, 0)), kv_spec, kv_spec],
        out_shape=[jax.ShapeDtypeStruct((b, C_HEADS, t, LANES), BF16),
                   jax.ShapeDtypeStruct((b, t, LANES), BF16),
                   jax.ShapeDtypeStruct((b, t, LANES), BF16)],
        compiler_params=_cp(('parallel', 'parallel')),
        name='gqa_prep',
    )(p, p, p, cos, sin, qg, kg, bd)


def _softmax_pv(q, k, v, scale):
    s = _dot_nt(q, k) * scale
    m = jnp.max(s, axis=-1, keepdims=True)
    e = jnp.exp(s - m)
    l = jnp.sum(e, axis=-1, keepdims=True)
    return _dot(e.astype(BF16), v) / l


def _gqa_attn_kernel(q_ref, k_ref, v_ref, o_ref, *, tq, t, n_ctx):
    grp = C_HEADS // C_KV_HEADS
    lane = lax.broadcasted_iota(jnp.int32, (1, LANES), 1)

    def attend(nk):
        k, v = k_ref[0, 0:nk, :], v_ref[0, 0:nk, :]
        for g in range(C_KV_HEADS):
            q = q_ref[0, g * grp:(g + 1) * grp].reshape(grp * tq, LANES)
            o = _softmax_pv(q, k, v, C_DH ** -0.5)
            for pair in range(grp // 2):
                o_even = o[(2 * pair) * tq:(2 * pair + 1) * tq]
                o_odd = o[(2 * pair + 1) * tq:(2 * pair + 2) * tq]
                lo = o_even if g == 0 else pltpu.roll(o_even, LANES // 2, 1)
                hi = pltpu.roll(o_odd, LANES // 2, 1) if g == 0 else o_odd
                col = (g * grp // 2 + pair) * LANES
                o_ref[0, :, col:col + LANES] = jnp.where(lane < LANES // 2, lo, hi)

    is_ctx = pl.program_id(1) * tq < n_ctx
    pl.when(is_ctx)(lambda: attend(n_ctx))
    pl.when(jnp.logical_not(is_ctx))(lambda: attend(t))


def _gqa_attn_call(q, k, v, n_ctx):
    b, _, t, _ = q.shape
    tq = _pick(n_ctx, (128,))
    kv_spec = pl.BlockSpec((1, t, LANES), lambda bi, i: (bi, 0, 0))
    return pl.pallas_call(
        functools.partial(_gqa_attn_kernel, tq=tq, t=t, n_ctx=n_ctx),
        grid=(b, t // tq),
        in_specs=[pl.BlockSpec((1, C_HEADS, tq, LANES), lambda bi, i: (bi, 0, i, 0)), kv_spec, kv_spec],
        out_specs=pl.BlockSpec((1, tq, C_HEADS * C_DH), lambda bi, i: (bi, i, 0)),
        out_shape=jax.ShapeDtypeStruct((b, t, C_HEADS * C_DH), F32),
        compiler_params=_cp(('parallel', 'parallel')),
        name='gqa_attn',
    )(q, k, v)


def _mla_prep_kernel(ckv_ref, misc_ref, cq_ref, cos_ref, sin_ref, kvg_ref, qg_ref,
                     wuk_ref, wuv_ref, wqn_ref, wqr_ref, q_ref, k_ref, v_ref):
    cos, sin = cos_ref[...], sin_ref[...]
    ckv = _rms(ckv_ref[0], kvg_ref[...]).astype(BF16)
    k_nope = _dot(ckv, wuk_ref[...])
    v_ref[0] = _dot(ckv, wuv_ref[...]).astype(BF16)
    k_rope = _half_to(_rope_tile(misc_ref[0], cos, sin), False, False).astype(BF16)
    cq = _rms(cq_ref[0], qg_ref[...]).astype(BF16)
    q_nope = _dot(cq, wqn_ref[...])
    q_rope = _dot(cq, wqr_ref[...])
    for mt in range(D_HEADS // 2):
        qr = _rope_tile(q_rope[:, mt * LANES:(mt + 1) * LANES], cos, sin)
        for half in range(2):
            h = 2 * mt + half
            q_ref[0, h, :, 0:D_NOPE] = q_nope[:, h * D_NOPE:(h + 1) * D_NOPE].astype(BF16)
            q_ref[0, h, :, D_NOPE:2 * D_NOPE] = _half_to(qr, half == 1, False).astype(BF16)
            k_ref[0, h, :, 0:D_NOPE] = k_nope[:, h * D_NOPE:(h + 1) * D_NOPE].astype(BF16)
            k_ref[0, h, :, D_NOPE:2 * D_NOPE] = k_rope


def _mla_prep_call(p, cos, sin, kvg, qg, wuk, wuv, wqn, wqr):
    b, t, _ = p.shape
    tm = _pick(t, (256, 128))
    tok = lambda blk, w: pl.BlockSpec((1, tm, w), lambda bi, i: (bi, i, blk))
    tab = pl.BlockSpec((tm, LANES), lambda bi, i: (i, 0))
    full = lambda a: pl.BlockSpec(a.shape, lambda bi, i: (0,) * a.ndim)
    hd = 2 * D_NOPE
    qk_spec = pl.BlockSpec((1, D_HEADS, tm, hd), lambda bi, i: (bi, 0, i, 0))
    return pl.pallas_call(
        _mla_prep_kernel,
        grid=(b, t // tm),
        in_specs=[tok(BLK['d_ckv'], LANES), tok(BLK['misc0'], LANES), tok(BLK['d_cq'] // 2, 2 * LANES),
                  tab, tab, full(kvg), full(qg), full(wuk), full(wuv), full(wqn), full(wqr)],
        out_specs=[qk_spec, qk_spec, pl.BlockSpec((1, tm, D_HEADS * D_DV), lambda bi, i: (bi, i, 0))],
        out_shape=[jax.ShapeDtypeStruct((b, D_HEADS, t, hd), BF16),
                   jax.ShapeDtypeStruct((b, D_HEADS, t, hd), BF16),
                   jax.ShapeDtypeStruct((b, t, D_HEADS * D_DV), BF16)],
        compiler_params=_cp(('parallel', 'parallel')),
        name='mla_prep',
    )(p, p, p, cos, sin, kvg, qg, wuk, wuv, wqn, wqr)


def _mla_attn_kernel(q_ref, k_ref, v_ref, o_ref, *, tq, t, n_ctx):
    def attend(nk):
        o_ref[0] = _softmax_pv(q_ref[0, 0], k_ref[0, 0, 0:nk, :], v_ref[0, 0:nk, :],
                               (D_NOPE + D_ROPE) ** -0.5)

    is_ctx = pl.program_id(2) * tq < n_ctx
    pl.when(is_ctx)(lambda: attend(n_ctx))
    pl.when(jnp.logical_not(is_ctx))(lambda: attend(t))


def _mla_attn_call(q, k, v, n_ctx):
    b, _, t, hd = q.shape
    tq = _pick(n_ctx, (256, 128))
    return pl.pallas_call(
        functools.partial(_mla_attn_kernel, tq=tq, t=t, n_ctx=n_ctx),
        grid=(b, D_HEADS, t // tq),
        in_specs=[pl.BlockSpec((1, 1, tq, hd), lambda bi, h, i: (bi, h, i, 0)),
                  pl.BlockSpec((1, 1, t, hd), lambda bi, h, i: (bi, h, 0, 0)),
                  pl.BlockSpec((1, t, D_DV), lambda bi, h, i: (bi, 0, h))],
        out_specs=pl.BlockSpec((1, tq, D_DV), lambda bi, h, i: (bi, i, h)),
        out_shape=jax.ShapeDtypeStruct((b, t, D_HEADS * D_DV), F32),
        compiler_params=_cp(('parallel', 'parallel', 'parallel')),
        name='mla_attn',
    )(q, k, v)


def _merge_kernel(ya_ref, yb_ref, yc_ref, yd_ref, g0_ref, g1_ref, g2_ref, g3_ref, x_ref, modb_ref, modc_ref,
                  wb_ref, wo_ref, o_ref, *, tm, n_ctx):
    acc = None
    for r, (y_ref, g_ref) in enumerate(((ya_ref, g0_ref), (yb_ref, g1_ref), (yc_ref, g2_ref), (yd_ref, g3_ref))):
        term = _sigmoid(g_ref[0]) * _dot(y_ref[0].astype(BF16), wb_ref[r])
        acc = term if acc is None else acc + term
    z = _dot(acc.astype(BF16), wo_ref[...])
    gate = _res_gate(modb_ref, modc_ref, pl.program_id(1) * tm, tm, n_ctx, 2)
    o_ref[0] = x_ref[0] + gate * z


def _merge_call(ys, p, x, modb, modc, wb, wo, n_ctx):
    b, t, d = x.shape
    tm = _pick(t, (256, 128))
    y_spec = pl.BlockSpec((1, tm, BRANCH_W), lambda bi, i: (bi, i, 0))
    g_specs = [pl.BlockSpec((1, tm, d), lambda bi, i, r=r: (bi, i, BLK['gates'] // 8 + r)) for r in range(N_BRANCH)]
    x_spec = pl.BlockSpec((1, tm, d), lambda bi, i: (bi, i, 0))
    return pl.pallas_call(
        functools.partial(_merge_kernel, tm=tm, n_ctx=n_ctx),
        grid=(b, t // tm),
        in_specs=[y_spec] * 4 + g_specs + [
            x_spec, pl.BlockSpec((1, 6, d), lambda bi, i: (bi, 0, 0)), pl.BlockSpec((6, d), lambda bi, i: (0, 0)),
            pl.BlockSpec((N_BRANCH, BRANCH_W, d), lambda bi, i: (0, 0, 0)),
            pl.BlockSpec((d, d), lambda bi, i: (0, 0))],
        out_specs=x_spec,
        out_shape=jax.ShapeDtypeStruct((b, t, d), F32),
        compiler_params=_cp(('parallel', 'parallel')),
        name='merge',
    )(*ys, p, p, p, p, x, modb, modc, wb, wo)


def _ffn_kernel(x_ref, modb_ref, modc_ref, g_ref, w1_ref, w2_ref, o_ref, h_ref, acc_ref, *, tm, n_ctx):
    j = pl.program_id(2)
    row0 = pl.program_id(1) * tm

    @pl.when(j == 0)
    def _():
        h_ref[...] = _mod_norm(x_ref, g_ref, modb_ref, modc_ref, row0, tm, n_ctx, 3, 4).astype(BF16)

    u = jnp.square(jnp.maximum(_dot(h_ref[...], w1_ref[...]), 0.0))
    part = _dot(u.astype(BF16), w2_ref[...])

    @pl.when(j == 0)
    def _():
        acc_ref[...] = part

    @pl.when(j > 0)
    def _():
        acc_ref[...] = acc_ref[...] + part

    @pl.when(j == pl.num_programs(2) - 1)
    def _():
        o_ref[0] = x_ref[0] + _res_gate(modb_ref, modc_ref, row0, tm, n_ctx, 5) * acc_ref[...]


def _ffn_call(x, modb, modc, g, w1, w2, n_ctx):
    b, t, d = x.shape
    tm = _pick(t, (768, 512, 384, 256, 128))
    tf = 1024
    x_spec = pl.BlockSpec((1, tm, d), lambda bi, i, j: (bi, i, 0))
    return pl.pallas_call(
        functools.partial(_ffn_kernel, tm=tm, n_ctx=n_ctx),
        grid=(b, t // tm, D_FF // tf),
        in_specs=[x_spec, pl.BlockSpec((1, 6, d), lambda bi, i, j: (bi, 0, 0)),
                  pl.BlockSpec((6, d), lambda bi, i, j: (0, 0)), pl.BlockSpec((1, d), lambda bi, i, j: (0, 0)),
                  pl.BlockSpec((d, tf), lambda bi, i, j: (0, j)), pl.BlockSpec((tf, d), lambda bi, i, j: (j, 0))],
        out_specs=x_spec,
        out_shape=jax.ShapeDtypeStruct((b, t, d), F32),
        scratch_shapes=[pltpu.VMEM((tm, d), BF16), pltpu.VMEM((tm, d), F32)],
        compiler_params=_cp(('parallel', 'parallel', 'arbitrary')),
        name='ffn',
    )(x, modb, modc, g, w1, w2)


def _final_kernel(x_ref, g_ref, o_ref):
    o_ref[0] = _rms(x_ref[0], g_ref[...])


def _final_call(x, g, n_ctx):
    b, t, d = x.shape
    tm = _pick(n_ctx, (256, 128))
    off = n_ctx // tm
    return pl.pallas_call(
        _final_kernel,
        grid=(b, (t - n_ctx) // tm),
        in_specs=[pl.BlockSpec((1, tm, d), lambda bi, i: (bi, i + off, 0)), pl.BlockSpec((1, d), lambda bi, i: (0, 0))],
        out_specs=pl.BlockSpec((1, tm, d), lambda bi, i: (bi, i, 0)),
        out_shape=jax.ShapeDtypeStruct((b, t - n_ctx, d), F32),
        compiler_params=_cp(('parallel', 'parallel')),
        name='final_norm',
    )(x, g)


def _pack_w_in(w_in):
    depth, d, _ = w_in.shape
    ref = lambda n: w_in[:, :, _REF_OFF[n][0]:_REF_OFF[n][0] + _REF_OFF[n][1]]
    zeros = lambda w: jnp.zeros((depth, d, w), w_in.dtype)
    gates = ref('b_gates').reshape(depth, d, 4, B_HEADS)
    pair_gates = [gates[:, :, :, 2 * hp:2 * hp + 2].reshape(depth, d, 8) for hp in range(2)]
    misc0 = jnp.concatenate([ref('d_krope'), pair_gates[0], zeros(LANES - 72)], axis=-1)
    misc1 = jnp.concatenate([zeros(64), pair_gates[1], zeros(LANES - 72)], axis=-1)
    parts = [ref('gates'), ref('a_i'), ref('a_f_fwd'), ref('a_f_bwd'), ref('a_q'), ref('a_g'), ref('b_v'),
             ref('b_o'), ref('c_q'), ref('b_k'), ref('b_q'), ref('d_cq'), ref('c_k'), ref('c_v'), ref('d_ckv'),
             misc0, misc1]
    packed = jnp.concatenate(parts, axis=-1)
    assert packed.shape[-1] == P_WIDTH
    return packed.astype(BF16)


def _rope_tables(n_lat, n_ctx):
    rows = n_lat // GRID_W
    row = jnp.repeat(jnp.arange(rows, dtype=jnp.int32), GRID_W)
    col = jnp.broadcast_to(jnp.arange(GRID_W, dtype=jnp.int32), (rows, GRID_W)).reshape(-1)
    quarter = C_DH // 4
    inv_freq = ROPE_THETA ** (-jnp.arange(quarter, dtype=F32) / quarter)
    ang_r = row.astype(F32)[:, None] * inv_freq
    ang_c = col.astype(F32)[:, None] * inv_freq
    cr, sr, cc, sc = jnp.cos(ang_r), jnp.sin(ang_r), jnp.cos(ang_c), jnp.sin(ang_c)
    cos = jnp.concatenate([cr, cr, cc, cc], axis=-1)
    sin = jnp.concatenate([-sr, sr, -sc, sc], axis=-1)
    cos = jnp.concatenate([jnp.ones((n_ctx, C_DH), F32), cos], axis=0)
    sin = jnp.concatenate([jnp.zeros((n_ctx, C_DH), F32), sin], axis=0)
    return jnp.tile(cos, (1, 2)), jnp.tile(sin, (1, 2))


def kernel(x, c, ctx, c_ctx, w_ada, b_ada, g_norm1, g_norm2, w_in, b_mlstm_gates, hgrn_lb_logits,
           hgrn_norm_g, mlstm_conv_w, mlstm_norm_g, gqa_q_norm_g, gqa_k_norm_g, mla_q_norm_g,
           mla_kv_norm_g, w_mla_uq, w_mla_uk, w_mla_uv, w_branch, w_out, w_ff1, w_ff2, g_final):
    bsz, n_lat, d = x.shape
    n_ctx = ctx.shape[1]
    depth = w_ada.shape[0]
    assert d == D_MODEL and n_lat % GRID_W == 0 and n_ctx % 128 == 0 and n_lat % 128 == 0

    w_in_p = _pack_w_in(w_in)
    w_ada_b = w_ada.astype(BF16)
    wb_b, wo_b = w_branch.astype(BF16), w_out.astype(BF16)
    w1_b, w2_b = w_ff1.astype(BF16), w_ff2.astype(BF16)
    wuk_b, wuv_b = w_mla_uk.astype(BF16), w_mla_uv.astype(BF16)
    uq = w_mla_uq.reshape(depth, D_Q_LORA, D_HEADS, D_NOPE + D_ROPE)
    wqn_b = uq[..., :D_NOPE].reshape(depth, D_Q_LORA, D_HEADS * D_NOPE).astype(BF16)
    wqr_b = uq[..., D_NOPE:].reshape(depth, D_Q_LORA, D_HEADS * D_ROPE).astype(BF16)
    cos_t, sin_t = _rope_tables(n_lat, n_ctx)
    half = lax.broadcasted_iota(jnp.int32, (LANES, LANES), 0) // C_DH == lax.broadcasted_iota(
        jnp.int32, (LANES, LANES), 1) // C_DH
    bd = jnp.where(half, 1.0 / C_DH, 0.0).astype(BF16)
    lb = jnp.cumsum(jax.nn.softmax(hgrn_lb_logits.astype(F32), axis=0), axis=0)
    lb = lb - lb[0]
    llb, l1m = jnp.log(lb), jnp.log1p(-lb)
    gb = b_mlstm_gates.reshape(depth, 4, B_HEADS)
    gate_bias = jnp.zeros((depth, 2, 1, LANES), F32)
    for hp in range(2):
        gate_bias = gate_bias.at[:, hp, 0, GATE_LANE0:GATE_LANE0 + 8].set(gb[:, :, 2 * hp:2 * hp + 2].reshape(depth, 8))
    tile2 = lambda g: jnp.tile(g, (1, 2))

    pad = (-(bsz + 1)) % 8
    s_rows = jnp.concatenate([c, c_ctx[None, :], jnp.zeros((pad, d), F32)], axis=0)
    mods = _ada_call(s_rows, w_ada_b, b_ada)

    xa = jnp.concatenate([ctx, x], axis=1)
    for l in range(depth):
        modb = mods[l, :bsz].reshape(bsz, 6, d)
        modc = mods[l, bsz].reshape(6, d)
        p = _inproj_call(xa, modb, modc, g_norm1[l][None, :], w_in_p[l], n_ctx)
        ya = _hgrn_call(p, llb[l], l1m[l], hgrn_norm_g[l][None, :], n_ctx)
        yb = _mlstm_call(p, mlstm_conv_w[l], gate_bias[l], mlstm_norm_g[l][None, :], n_ctx)
        qc, kc, vc = _gqa_prep_call(p, cos_t, sin_t, tile2(gqa_q_norm_g[l][None, :]),
                                    tile2(gqa_k_norm_g[l][None, :]), bd)
        yc = _gqa_attn_call(qc, kc, vc, n_ctx)
        qd, kd, vd = _mla_prep_call(p, cos_t, sin_t, mla_kv_norm_g[l][None, :], mla_q_norm_g[l][None, :],
                                    wuk_b[l], wuv_b[l], wqn_b[l], wqr_b[l])
        yd = _mla_attn_call(qd, kd, vd, n_ctx)
        xa = _merge_call((ya, yb, yc, yd), p, xa, modb, modc, wb_b[l], wo_b[l], n_ctx)
        xa = _ffn_call(xa, modb, modc, g_norm2[l][None, :], w1_b[l], w2_b[l], n_ctx)
    return _final_call(xa, g_final[None, :], n_ctx)
```

```python
import functools

import numpy as np
import jax
import jax.numpy as jnp
from jax import lax
from jax.experimental import pallas as pl
from jax.experimental.pallas import tpu as pltpu

F32 = jnp.float32
BF16 = jnp.bfloat16

D_MODEL = 1024
GRID_W = 64
EPS = 1e-6
NEG = -1e30
ROPE_THETA = 10000.0
CHUNK = 64
LANES = 128

A_HEADS, A_DK, A_DV = 4, 128, 128
B_HEADS, B_DQK, B_DV, B_CONV = 4, 64, 128, 3
C_HEADS, C_KV_HEADS, C_DH = 8, 2, 64
D_HEADS, D_Q_LORA, D_KV_LORA, D_NOPE, D_ROPE, D_DV = 4, 256, 128, 128, 64, 128
N_BRANCH, BRANCH_W = 4, 512
D_FF = 4 * D_MODEL

_REF_COLS = (
    ('a_i', 512), ('a_f_fwd', 512), ('a_f_bwd', 512), ('b_k', 256), ('b_v', 512), ('b_gates', 16),
    ('c_k', 128), ('c_v', 128), ('d_ckv', 128), ('d_krope', 64),
    ('a_q', 512), ('a_g', 512), ('b_q', 256), ('b_o', 512), ('c_q', 512), ('d_cq', 256), ('gates', 4096),
)
_REF_OFF = {}
_o = 0
for _n, _w in _REF_COLS:
    _REF_OFF[_n] = (_o, _w)
    _o += _w

BLK = dict(gates=0, a_i=32, a_ff=36, a_fb=40, a_q=44, a_g=48, b_v=52, b_o=56, c_q=60, b_k=64, b_q=66,
           d_cq=68, c_k=70, c_v=71, d_ckv=72, misc0=73, misc1=74)
P_BLOCKS = 75
P_WIDTH = P_BLOCKS * LANES
GATE_LANE0 = 64

VMEM_LIMIT = 56 * 1024 * 1024


def _cp(sem, **kw):
    return pltpu.CompilerParams(dimension_semantics=sem, vmem_limit_bytes=VMEM_LIMIT, **kw)


def _pick(n, cands):
    for c in cands:
        if n % c == 0:
            return c
    raise ValueError(f'no tile for {n}')


def _sigmoid(x):
    return 1.0 / (1.0 + jnp.exp(-x))


def _silu(x):
    return x * _sigmoid(x)


def _softplus_neg_abs(x):
    return jnp.log(1.0 + jnp.exp(-jnp.abs(x)))


def _rms(xf, g):
    ms = jnp.mean(xf * xf, axis=-1, keepdims=True)
    return xf * lax.rsqrt(ms + EPS) * g


def _log_sigmoid(x):
    return jnp.minimum(x, 0.0) - _softplus_neg_abs(x)


def _logaddexp(a, b):
    return jnp.maximum(a, b) + _softplus_neg_abs(a - b)


def _dot(a, b):
    return jnp.dot(a, b, preferred_element_type=F32)


def _dot_nt(a, b):
    return lax.dot_general(a, b, (((1,), (1,)), ((), ())), preferred_element_type=F32)


def _dot_tn(a, b):
    return lax.dot_general(a, b, (((0,), (0,)), ((), ())), preferred_element_type=F32)


def _mod_norm(x_ref, g_ref, modb_ref, modc_ref, row0, tm, n_ctx, k_shift, k_scale):
    y = _rms(x_ref[0], g_ref[...])
    row = row0 + lax.broadcasted_iota(jnp.int32, (tm, 1), 0)
    is_ctx = row < n_ctx
    shift = jnp.where(is_ctx, modc_ref[k_shift:k_shift + 1, :], modb_ref[0, k_shift:k_shift + 1, :])
    scale = jnp.where(is_ctx, modc_ref[k_scale:k_scale + 1, :], modb_ref[0, k_scale:k_scale + 1, :])
    return y * (1.0 + scale) + shift


def _res_gate(modb_ref, modc_ref, row0, tm, n_ctx, k):
    row = row0 + lax.broadcasted_iota(jnp.int32, (tm, 1), 0)
    return jnp.where(row < n_ctx, modc_ref[k:k + 1, :], modb_ref[0, k:k + 1, :])


def _ada_kernel(s_ref, w_ref, b_ref, o_ref):
    s = _silu(s_ref[...])
    o_ref[0] = _dot(s.astype(BF16), w_ref[0]) + b_ref[0]


def _ada_call(s_rows, w_ada, b_ada):
    depth, d, d6 = w_ada.shape
    r = s_rows.shape[0]
    return pl.pallas_call(
        _ada_kernel,
        grid=(depth, d6 // d),
        in_specs=[pl.BlockSpec((r, d), lambda l, j: (0, 0)),
                  pl.BlockSpec((1, d, d), lambda l, j: (l, 0, j)),
                  pl.BlockSpec((1, 1, d), lambda l, j: (l, 0, j))],
        out_specs=pl.BlockSpec((1, r, d), lambda l, j: (l, 0, j)),
        out_shape=jax.ShapeDtypeStruct((depth, r, d6), F32),
        compiler_params=_cp(('parallel', 'parallel')),
        name='ada',
    )(s_rows, w_ada, b_ada.reshape(depth, 1, d6))


def _inproj_kernel(x_ref, modb_ref, modc_ref, g_ref, w_ref, o_ref, h_ref, *, tm, n_ctx):
    @pl.when(pl.program_id(2) == 0)
    def _():
        h = _mod_norm(x_ref, g_ref, modb_ref, modc_ref, pl.program_id(1) * tm, tm, n_ctx, 0, 1)
        h_ref[...] = h.astype(BF16)

    o_ref[0] = _dot(h_ref[...], w_ref[...])


def _inproj_call(x, modb, modc, g, w, n_ctx):
    b, t, d = x.shape
    tm = _pick(t, (768, 512, 384, 256, 128))
    tn = _pick(P_WIDTH, (1920, 1280, 640))
    return pl.pallas_call(
        functools.partial(_inproj_kernel, tm=tm, n_ctx=n_ctx),
        grid=(b, t // tm, P_WIDTH // tn),
        in_specs=[pl.BlockSpec((1, tm, d), lambda bi, i, j: (bi, i, 0)),
                  pl.BlockSpec((1, 6, d), lambda bi, i, j: (bi, 0, 0)),
                  pl.BlockSpec((6, d), lambda bi, i, j: (0, 0)),
                  pl.BlockSpec((1, d), lambda bi, i, j: (0, 0)),
                  pl.BlockSpec((d, tn), lambda bi, i, j: (0, j))],
        out_specs=pl.BlockSpec((1, tm, tn), lambda bi, i, j: (bi, i, j)),
        out_shape=jax.ShapeDtypeStruct((b, t, P_WIDTH), F32),
        scratch_shapes=[pltpu.VMEM((tm, d), BF16)],
        compiler_params=_cp(('parallel', 'parallel', 'arbitrary')),
        name='inproj',
    )(x, modb, modc, g, w)


def _chunk_of_step(step, n_chunks, n_ctx_chunks, rev):
    if not rev:
        return step
    return jnp.where(step < n_ctx_chunks, n_ctx_chunks - 1 - step, n_chunks + n_ctx_chunks - 1 - step)


def _order_mask(n, rev):
    ti = lax.broadcasted_iota(jnp.int32, (n, n), 0)
    si = lax.broadcasted_iota(jnp.int32, (n, n), 1)
    return (si >= ti) if rev else (si <= ti)


_BLOCK_W = (2, 4, 8, 16, 32, 64)
_N_LEVELS = 7
A_HPS = 2


def _hgrn_tables():
    ti, si = np.meshgrid(np.arange(CHUNK), np.arange(CHUNK), indexing='ij')
    mats = [np.tile((ti // w == si // w) & (si <= ti), (1, 4)) for w in _BLOCK_W]
    lvl = np.full((2, CHUNK, CHUNK), -1, np.int32)
    for d in range(2):
        lvl[d][ti == si] = 0
        for i, w in enumerate((1, 2, 4, 8, 16, 32)):
            same = ti // (2 * w) == si // (2 * w)
            t_late = (ti // w) % 2 == (1 - d)
            s_early = (si // w) % 2 == d
            lvl[d][same & t_late & s_early] = i + 1
    return np.concatenate(mats, axis=0).astype(np.float32), lvl


def _block_total(p, w):
    n = p.shape[0]
    if w >= 8:
        parts = [jnp.broadcast_to(p[e - 1:e, :], (w, p.shape[1])) for e in range(w, n + 1, w)]
        return parts[0] if len(parts) == 1 else jnp.concatenate(parts, axis=0)
    x = p.reshape(n // 8, 8, p.shape[1])
    sub = lax.broadcasted_iota(jnp.int32, (1, 8, 1), 1)
    out = jnp.broadcast_to(x[:, 7:8, :], x.shape)
    for e in range(8 - w - 1, -1, -w):
        out = jnp.where(sub <= e, jnp.broadcast_to(x[:, e:e + 1, :], x.shape), out)
    return out.reshape(n, p.shape[1])


def _hgrn_kernel(v_ref, ff_ref, fb_ref, q_ref, g_ref, llb_ref, l1m_ref, ng_ref, bs_ref, lvl_ref, o_ref,
                 accf_ref, accb_ref, qs_ref, *, n_chunks, n_ctx_chunks):
    n = CHUNK
    nw = len(_BLOCK_W)
    f_refs, acc_refs = (ff_ref, fb_ref), (accf_ref, accb_ref)
    streams = [(hh, d) for hh in range(A_HPS) for d in range(2)]
    lanes = lambda hh: slice(hh * A_DK, (hh + 1) * A_DK)

    def body(step, carry):
        rows = []
        for d in range(2):
            c = _chunk_of_step(step, n_chunks, n_ctx_chunks, d == 1)
            rows.append(pl.ds(pl.multiple_of(c * n, n), n))
        lf, kk, q, v, sums = {}, {}, {}, {}, {}
        for hh, d in streams:
            fpre = f_refs[d][0, rows[d], lanes(hh)]
            l1m = l1m_ref[d:d + 1, lanes(hh)]
            softplus = _softplus_neg_abs(fpre)
            lf[hh, d] = _logaddexp(llb_ref[d:d + 1, lanes(hh)], l1m + (jnp.minimum(fpre, 0.0) - softplus))
            kk[hh, d] = jnp.exp(l1m + (jnp.minimum(-fpre, 0.0) - softplus))
            q[hh, d] = qs_ref[rows[d], lanes(hh)]
            v[hh, d] = v_ref[0, rows[d], lanes(hh)].astype(BF16)
        for hh in range(A_HPS):
            parts = []
            for d in range(2):
                hi = lf[hh, d].astype(BF16)
                lo = (lf[hh, d] - hi.astype(F32)).astype(BF16)
                zero = jnp.zeros_like(hi)
                parts += [jnp.concatenate([hi, zero] if d == 0 else [zero, hi], axis=1),
                          jnp.concatenate([lo, zero] if d == 0 else [zero, lo], axis=1)]
            sums_fb = _dot(bs_ref[...], jnp.concatenate(parts, axis=0))
            sums[hh, 0], sums[hh, 1] = sums_fb[:, 0:A_DK], sums_fb[:, A_DK:2 * A_DK]

        a_lv, o_inter, st_new = {}, {}, []
        for si, (hh, d) in enumerate(streams):
            key = (hh, d)
            pre = lambda i, key=key: sums[key][i * n:(i + 1) * n]
            rest = lambda i, pre=pre: _block_total(pre(i), _BLOCK_W[i]) - pre(i)
            if d == 0:
                eq, ek = pre, rest
            else:
                eq = lambda i, key=key, rest=rest: rest(i) + lf[key]
                ek = lambda i, key=key, pre=pre: pre(i) - lf[key]
            kb = kk[key].astype(BF16)
            q01 = jnp.concatenate([q[key], q[key] * jnp.exp(lf[key])], axis=0).astype(BF16)
            a01 = _dot_nt(q01, kb)
            mats = [a01[0:n], a01[n:2 * n]]
            for i in range(nw - 1):
                ql = (q[key] * jnp.exp(eq(i))).astype(BF16)
                kl = (kk[key] * jnp.exp(ek(i))).astype(BF16)
                mats.append(_dot_nt(ql, kl))
            a_lv[key] = mats
            st = carry[si]
            o_inter[key] = _dot_nt((q[key] * jnp.exp(eq(nw - 1))).astype(BF16), st.astype(BF16))
            kdec = (kk[key] * jnp.exp(ek(nw - 1))).astype(BF16)
            total = sums[key][(nw - 1) * n + n - 1:(nw - 1) * n + n]
            st_new.append(jnp.exp(total) * st + _dot_tn(v[key], kdec))

        o_intra = {}
        for key in streams:
            lvl = lvl_ref[key[1]]
            a = jnp.zeros((n, n), F32)
            for i in range(_N_LEVELS):
                a = jnp.where(lvl == i, a_lv[key][i], a)
            o_intra[key] = _dot(a.astype(BF16), v[key])
        for hh, d in streams:
            acc_refs[d][rows[d], lanes(hh)] = o_inter[hh, d] + o_intra[hh, d]
        return tuple(st_new)

    qs_ref[...] = _silu(q_ref[0])
    zero = jnp.zeros((A_DV, A_DK), F32)
    lax.fori_loop(0, n_chunks, body, (zero,) * len(streams), unroll=4 if n_chunks % 4 == 0 else 2)
    for hh in range(A_HPS):
        o = accf_ref[:, lanes(hh)] + accb_ref[:, lanes(hh)]
        o_ref[0, :, lanes(hh)] = _rms(o, ng_ref[...]) * _silu(g_ref[0, :, lanes(hh)])


def _hgrn_call(p, llb, l1m, ng, n_ctx):
    b, t, _ = p.shape
    bs_np, lvl_np = _hgrn_tables()
    bs, lvl = jnp.asarray(bs_np, BF16), jnp.asarray(lvl_np)
    wid = A_HPS * A_DK
    col = lambda base: pl.BlockSpec((1, t, wid), lambda bi, h: (bi, 0, base // A_HPS + h))
    par = pl.BlockSpec((2, wid), lambda bi, h: (0, h))
    return pl.pallas_call(
        functools.partial(_hgrn_kernel, n_chunks=t // CHUNK, n_ctx_chunks=n_ctx // CHUNK),
        grid=(b, A_HEADS // A_HPS),
        in_specs=[col(BLK['a_i']), col(BLK['a_ff']), col(BLK['a_fb']), col(BLK['a_q']), col(BLK['a_g']),
                  par, par, pl.BlockSpec((1, A_DV), lambda bi, h: (0, 0)),
                  pl.BlockSpec(bs.shape, lambda bi, h: (0, 0)), pl.BlockSpec(lvl.shape, lambda bi, h: (0, 0, 0))],
        out_specs=pl.BlockSpec((1, t, wid), lambda bi, h: (bi, 0, h)),
        out_shape=jax.ShapeDtypeStruct((b, t, A_HEADS * A_DV), F32),
        scratch_shapes=[pltpu.VMEM((t, wid), F32), pltpu.VMEM((t, wid), F32), pltpu.VMEM((t, wid), F32)],
        compiler_params=_cp(('parallel', 'parallel')),
        name='hgrn2',
    )(p, p, p, p, p, llb, l1m, ng, bs, lvl)


def _scan_max(x, rev):
    n = x.shape[0]
    row = lax.broadcasted_iota(jnp.int32, (n, 1), 0)
    k = 1
    while k < n:
        if rev:
            shifted = jnp.where(row < n - k, pltpu.roll(x, n - k, 0), NEG)
        else:
            shifted = jnp.where(row >= k, pltpu.roll(x, k, 0), NEG)
        x = jnp.maximum(x, shifted)
        k *= 2
    return x


def _lane_bcast(x, lane, width):
    return jnp.broadcast_to(x[:, lane:lane + 1], (x.shape[0], width))


def _mlstm_kernel(kp_ref, qp_ref, v_ref, og_ref, misc_ref, cw_ref, gb_ref, ng_ref, tri_ref, o_ref,
                  qs_ref, ks_ref, accf_ref, accb_ref, c_ref, *, t, n_ctx):
    n = CHUNK
    n_chunks, n_ctx_chunks = t // n, n_ctx // n
    row = lax.broadcasted_iota(jnp.int32, (t, 1), 0)
    no_prev = (row == 0) | (row == n_ctx)
    no_next = (row == n_ctx - 1) | (row == t - 1)

    def conv_silu(x, w):
        xm = jnp.where(no_prev, 0.0, pltpu.roll(x, 1, 0))
        xp = jnp.where(no_next, 0.0, pltpu.roll(x, t - 1, 0))
        return _silu(w[0:1, :] * xm + w[1:2, :] * x + w[2:3, :] * xp)

    qs_ref[...] = conv_silu(qp_ref[0], cw_ref[0])
    ks_ref[...] = conv_silu(kp_ref[0], cw_ref[1]) * (B_DQK ** -0.5)
    c_ref[...] = jnp.zeros(c_ref.shape, F32)

    lane = lax.broadcasted_iota(jnp.int32, (1, LANES), 1)
    head_lanes = (lane < B_DQK, lane >= B_DQK)
    row2 = lax.broadcasted_iota(jnp.int32, (2 * B_DQK, 1), 0)
    acc_refs = (accf_ref, accb_ref)
    streams = [(d, j) for d in range(2) for j in range(2)]
    gate_lane = lambda d, j: GATE_LANE0 + 4 * d + j

    def body(step, m_prev):
        rows, g_all, bal_all = [], [], []
        for d in range(2):
            c = _chunk_of_step(step, n_chunks, n_ctx_chunks, d == 1)
            rows.append(pl.ds(pl.multiple_of(c * n, n), n))
            g = misc_ref[0, rows[d], :] + gb_ref[0]
            ls = _log_sigmoid(g)
            hi = ls.astype(BF16)
            lo = (ls - hi.astype(F32)).astype(BF16)
            bc = _dot(tri_ref[d], hi) + _dot(tri_ref[d], lo)
            g_all.append(g)
            bal_all.append(pltpu.roll(bc, LANES - 2, 1))

        qm, kf, qk, qc = {}, {}, {}, {}
        for d, j in streams:
            kf[d] = ks_ref[rows[d], :]
            qm[d, j] = jnp.where(head_lanes[j], qs_ref[rows[d], :], 0.0).astype(BF16)
            qk[d, j] = _dot_nt(qm[d, j], kf[d].astype(BF16))
            qc[d, j] = _dot(qm[d, j], c_ref[d].astype(BF16))

        u_all, ut_all, cu_all, r_all, w_all, em_all, wk_all, dec_all, m_new = [], [], [], [], [], [], [], [], []
        for d in range(2):
            last = 0 if d == 1 else n - 1
            u = g_all[d] - bal_all[d]
            cu = _scan_max(u, d == 1)
            gm = jnp.maximum(m_prev[d], cu)
            w = jnp.exp(m_prev[d] - gm)
            gl = gm[last:last + 1]
            u_all.append(u)
            ut_all.append(jnp.concatenate([u, cu], axis=0).T[:, 0:n])
            cu_all.append(cu)
            r_all.append(jnp.exp(cu - gm))
            w_all.append(w)
            em_all.append(jnp.exp(-bal_all[d] - gm))
            wk_all.append(jnp.exp(u - gl))
            dec_all.append(w[last:last + 1])
            m_new.append(bal_all[d][last:last + 1] + gl)

        un, uk = {}, {}
        for d, j in streams:
            li = gate_lane(d, j)
            expo = ut_all[d][li:li + 1, :] - _lane_bcast(cu_all[d], li, n)
            s0 = (qk[d, j] * jnp.exp(jnp.where(_order_mask(n, d == 1), expo, NEG))).astype(BF16)
            one_col = jnp.broadcast_to(jnp.where(lane == li, 1.0, 0.0), (n, LANES))
            vaug = jnp.concatenate([v_ref[0, rows[d], j * B_DV:(j + 1) * B_DV], one_col], axis=1).astype(BF16)
            un[d, j] = _dot(s0, vaug)
            kw = jnp.where(head_lanes[j], kf[d], 0.0) * _lane_bcast(wk_all[d], li, LANES)
            uk[d, j] = _dot_tn(kw.astype(BF16), vaug)

        for d, j in streams:
            li = gate_lane(d, j)
            num = (_lane_bcast(w_all[d], li, B_DV) * qc[d, j][:, 0:B_DV]
                   + _lane_bcast(r_all[d], li, B_DV) * un[d, j][:, 0:B_DV])
            den = w_all[d] * qc[d, j][:, B_DV:] + r_all[d] * un[d, j][:, B_DV:]
            z = 1.0 / jnp.maximum(jnp.abs(den), em_all[d])
            acc_refs[d][rows[d], j * B_DV:(j + 1) * B_DV] = num * _lane_bcast(z, li, B_DV)
        for d in range(2):
            l0, l1 = gate_lane(d, 0), gate_lane(d, 1)
            dec = jnp.where(row2 < B_DQK, dec_all[d][:, l0:l0 + 1], dec_all[d][:, l1:l1 + 1])
            c_ref[d] = dec * c_ref[d] + uk[d, 0] + uk[d, 1]
        return tuple(m_new)

    m0 = jnp.full((1, LANES), NEG, F32)
    lax.fori_loop(0, n_chunks, body, (m0, m0), unroll=2)

    for j in range(2):
        cols = slice(j * B_DV, (j + 1) * B_DV)
        h = accf_ref[:, cols] + accb_ref[:, cols]
        o_ref[0, :, cols] = _rms(h, ng_ref[...]) * _sigmoid(og_ref[0, :, cols])


def _mlstm_call(p, conv_w, gate_bias, ng, n_ctx):
    b, t, _ = p.shape
    pairs = B_HEADS // 2
    tri = jnp.stack([_order_mask(CHUNK, False), _order_mask(CHUNK, True)]).astype(BF16)
    return pl.pallas_call(
        functools.partial(_mlstm_kernel, t=t, n_ctx=n_ctx),
        grid=(b, pairs),
        in_specs=[pl.BlockSpec((1, t, LANES), lambda bi, hp: (bi, 0, BLK['b_k'] + hp)),
                  pl.BlockSpec((1, t, LANES), lambda bi, hp: (bi, 0, BLK['b_q'] + hp)),
                  pl.BlockSpec((1, t, 2 * B_DV), lambda bi, hp: (bi, 0, BLK['b_v'] // 2 + hp)),
                  pl.BlockSpec((1, t, 2 * B_DV), lambda bi, hp: (bi, 0, BLK['b_o'] // 2 + hp)),
                  pl.BlockSpec((1, t, LANES), lambda bi, hp: (bi, 0, BLK['misc0'] + hp)),
                  pl.BlockSpec((2, B_CONV, LANES), lambda bi, hp: (0, 0, hp)),
                  pl.BlockSpec((1, 1, LANES), lambda bi, hp: (hp, 0, 0)),
                  pl.BlockSpec((1, LANES), lambda bi, hp: (0, 0)),
                  pl.BlockSpec((2, CHUNK, CHUNK), lambda bi, hp: (0, 0, 0))],
        out_specs=pl.BlockSpec((1, t, 2 * B_DV), lambda bi, hp: (bi, 0, hp)),
        out_shape=jax.ShapeDtypeStruct((b, t, B_HEADS * B_DV), F32),
        scratch_shapes=[pltpu.VMEM((t, LANES), F32), pltpu.VMEM((t, LANES), F32),
                        pltpu.VMEM((t, 2 * B_DV), F32), pltpu.VMEM((t, 2 * B_DV), F32),
                        pltpu.VMEM((2, 2 * B_DQK, 2 * B_DV), F32)],
        compiler_params=_cp(('parallel', 'parallel')),
        name='mlstm',
    )(p, p, p, p, p, conv_w, gate_bias, ng, tri)


def _rope_tile(x, cos, sin_signed):
    lane = lax.broadcasted_iota(jnp.int32, (1, LANES), 1)
    first = (lane // (C_DH // 4)) % 2 == 0
    partner = jnp.where(first, pltpu.roll(x, LANES - C_DH // 4, 1), pltpu.roll(x, C_DH // 4, 1))
    return x * cos + partner * sin_signed


def _head_norm_tile(x, g, bd):
    sq = x * x
    hi = sq.astype(BF16)
    lo = (sq - hi.astype(F32)).astype(BF16)
    ms = _dot(hi, bd) + _dot(lo, bd)
    return x * lax.rsqrt(ms + EPS) * g


def _half_to(x, src_hi, dst_hi):
    lane = lax.broadcasted_iota(jnp.int32, (1, LANES), 1)
    if src_hi != dst_hi:
        x = pltpu.roll(x, LANES // 2, 1)
    keep = (lane >= LANES // 2) if dst_hi else (lane < LANES // 2)
    return jnp.where(keep, x, 0.0)


def _gqa_prep_kernel(cq_ref, ck_ref, cv_ref, cos_ref, sin_ref, qg_ref, kg_ref, bd_ref,
                     q_ref, k_ref, v_ref):
    cos, sin, bd = cos_ref[...], sin_ref[...], bd_ref[...]
    k = _head_norm_tile(ck_ref[0], kg_ref[...], bd)
    k_ref[0] = _rope_tile(k, cos, sin).astype(BF16)
    v_ref[0] = cv_ref[0].astype(BF16)
    grp = C_HEADS // C_KV_HEADS
    for mt in range(C_HEADS // 2):
        x = _head_norm_tile(cq_ref[0, :, mt * LANES:(mt + 1) * LANES], qg_ref[...], bd)
        x = _rope_tile(x, cos, sin) * (C_DH ** -0.5)
        for half in range(2):
            head = 2 * mt + half
            q_ref[0, head] = _half_to(x, half == 1, head // grp == 1).astype(BF16)


def _gqa_prep_call(p, cos, sin, qg, kg, bd):
    b, t, _ = p.shape
    tm = _pick(t, (256, 128))
    tok = lambda blk, w: pl.BlockSpec((1, tm, w), lambda bi, i: (bi, i, blk))
    tab = pl.BlockSpec((tm, LANES), lambda bi, i: (i, 0))
    vec = pl.BlockSpec((1, LANES), lambda bi, i: (0, 0))
    kv_spec = pl.BlockSpec((1, tm, LANES), lambda bi, i: (bi, i, 0))
    return pl.pallas_call(
        _gqa_prep_kernel,
        grid=(b, t // tm),
        in_specs=[tok(BLK['c_q'] // 4, 4 * LANES), tok(BLK['c_k'], LANES), tok(BLK['c_v'], LANES),
                  tab, tab, vec, vec, pl.BlockSpec((LANES, LANES), lambda bi, i: (0, 0))],
        out_specs=[pl.BlockSpec((1, C_HEADS, tm, LANES), lambda bi, i: (bi, 0, i, 0)), kv_spec, kv_spec],
        out_shape=[jax.ShapeDtypeStruct((b, C_HEADS, t, LANES), BF16),
                   jax.ShapeDtypeStruct((b, t, LANES), BF16),
                   jax.ShapeDtypeStruct((b, t, LANES), BF16)],
        compiler_params=_cp(('parallel', 'parallel')),
        name='gqa_prep',
    )(p, p, p, cos, sin, qg, kg, bd)


def _softmax_pv(q, k, v, scale):
    s = _dot_nt(q, k)
    if scale is not None:
        s = s * scale
    m = jnp.max(s, axis=-1, keepdims=True)
    e = jnp.exp(s - m)
    l = jnp.sum(e, axis=-1, keepdims=True)
    return _dot(e.astype(BF16), v) / l


def _gqa_attn_kernel(q_ref, k_ref, v_ref, o_ref, *, tq, t, n_ctx):
    grp = C_HEADS // C_KV_HEADS
    lane = lax.broadcasted_iota(jnp.int32, (1, LANES), 1)

    def attend(nk):
        k, v = k_ref[0, 0:nk, :], v_ref[0, 0:nk, :]
        for g in range(C_KV_HEADS):
            q = q_ref[0, g * grp:(g + 1) * grp].reshape(grp * tq, LANES)
            o = _softmax_pv(q, k, v, None)
            for pair in range(grp // 2):
                o_even = o[(2 * pair) * tq:(2 * pair + 1) * tq]
                o_odd = o[(2 * pair + 1) * tq:(2 * pair + 2) * tq]
                lo = o_even if g == 0 else pltpu.roll(o_even, LANES // 2, 1)
                hi = pltpu.roll(o_odd, LANES // 2, 1) if g == 0 else o_odd
                col = (g * grp // 2 + pair) * LANES
                o_ref[0, :, col:col + LANES] = jnp.where(lane < LANES // 2, lo, hi)

    is_ctx = pl.program_id(1) * tq < n_ctx
    pl.when(is_ctx)(lambda: attend(n_ctx))
    pl.when(jnp.logical_not(is_ctx))(lambda: attend(t))


def _gqa_attn_call(q, k, v, n_ctx):
    b, _, t, _ = q.shape
    tq = _pick(n_ctx, (128,))
    kv_spec = pl.BlockSpec((1, t, LANES), lambda bi, i: (bi, 0, 0))
    return pl.pallas_call(
        functools.partial(_gqa_attn_kernel, tq=tq, t=t, n_ctx=n_ctx),
        grid=(b, t // tq),
        in_specs=[pl.BlockSpec((1, C_HEADS, tq, LANES), lambda bi, i: (bi, 0, i, 0)), kv_spec, kv_spec],
        out_specs=pl.BlockSpec((1, tq, C_HEADS * C_DH), lambda bi, i: (bi, i, 0)),
        out_shape=jax.ShapeDtypeStruct((b, t, C_HEADS * C_DH), F32),
        compiler_params=_cp(('parallel', 'parallel')),
        name='gqa_attn',
    )(q, k, v)


def _mla_prep_kernel(ckv_ref, misc_ref, cq_ref, cos_ref, sin_ref, kvg_ref, qg_ref,
                     wuk_ref, wuv_ref, wqn_ref, wqr_ref, q_ref, k_ref, v_ref):
    cos, sin = cos_ref[...], sin_ref[...]
    ckv = _rms(ckv_ref[0], kvg_ref[...]).astype(BF16)
    k_nope = _dot(ckv, wuk_ref[...])
    v_ref[0] = _dot(ckv, wuv_ref[...]).astype(BF16)
    k_rope = _half_to(_rope_tile(misc_ref[0], cos, sin), False, False).astype(BF16)
    cq = _rms(cq_ref[0], qg_ref[...]).astype(BF16)
    q_nope = _dot(cq, wqn_ref[...])
    q_rope = _dot(cq, wqr_ref[...])
    for mt in range(D_HEADS // 2):
        qr = _rope_tile(q_rope[:, mt * LANES:(mt + 1) * LANES], cos, sin)
        for half in range(2):
            h = 2 * mt + half
            q_ref[0, h, :, 0:D_NOPE] = q_nope[:, h * D_NOPE:(h + 1) * D_NOPE].astype(BF16)
            q_ref[0, h, :, D_NOPE:2 * D_NOPE] = _half_to(qr, half == 1, False).astype(BF16)
            k_ref[0, h, :, 0:D_NOPE] = k_nope[:, h * D_NOPE:(h + 1) * D_NOPE].astype(BF16)
            k_ref[0, h, :, D_NOPE:2 * D_NOPE] = k_rope


def _mla_prep_call(p, cos, sin, kvg, qg, wuk, wuv, wqn, wqr):
    b, t, _ = p.shape
    tm = _pick(t, (256, 128))
    tok = lambda blk, w: pl.BlockSpec((1, tm, w), lambda bi, i: (bi, i, blk))
    tab = pl.BlockSpec((tm, LANES), lambda bi, i: (i, 0))
    full = lambda a: pl.BlockSpec(a.shape, lambda bi, i: (0,) * a.ndim)
    hd = 2 * D_NOPE
    qk_spec = pl.BlockSpec((1, D_HEADS, tm, hd), lambda bi, i: (bi, 0, i, 0))
    return pl.pallas_call(
        _mla_prep_kernel,
        grid=(b, t // tm),
        in_specs=[tok(BLK['d_ckv'], LANES), tok(BLK['misc0'], LANES), tok(BLK['d_cq'] // 2, 2 * LANES),
                  tab, tab, full(kvg), full(qg), full(wuk), full(wuv), full(wqn), full(wqr)],
        out_specs=[qk_spec, qk_spec, pl.BlockSpec((1, tm, D_HEADS * D_DV), lambda bi, i: (bi, i, 0))],
        out_shape=[jax.ShapeDtypeStruct((b, D_HEADS, t, hd), BF16),
                   jax.ShapeDtypeStruct((b, D_HEADS, t, hd), BF16),
                   jax.ShapeDtypeStruct((b, t, D_HEADS * D_DV), BF16)],
        compiler_params=_cp(('parallel', 'parallel')),
        name='mla_prep',
    )(p, p, p, cos, sin, kvg, qg, wuk, wuv, wqn, wqr)


def _mla_attn_kernel(q_ref, k_ref, v_ref, o_ref, *, tq, t, n_ctx):
    def attend(nk):
        o_ref[0] = _softmax_pv(q_ref[0, 0], k_ref[0, 0, 0:nk, :], v_ref[0, 0:nk, :],
                               (D_NOPE + D_ROPE) ** -0.5)

    is_ctx = pl.program_id(2) * tq < n_ctx
    pl.when(is_ctx)(lambda: attend(n_ctx))
    pl.when(jnp.logical_not(is_ctx))(lambda: attend(t))


def _mla_attn_call(q, k, v, n_ctx):
    b, _, t, hd = q.shape
    tq = _pick(n_ctx, (256, 128))
    return pl.pallas_call(
        functools.partial(_mla_attn_kernel, tq=tq, t=t, n_ctx=n_ctx),
        grid=(b, D_HEADS, t // tq),
        in_specs=[pl.BlockSpec((1, 1, tq, hd), lambda bi, h, i: (bi, h, i, 0)),
                  pl.BlockSpec((1, 1, t, hd), lambda bi, h, i: (bi, h, 0, 0)),
                  pl.BlockSpec((1, t, D_DV), lambda bi, h, i: (bi, 0, h))],
        out_specs=pl.BlockSpec((1, tq, D_DV), lambda bi, h, i: (bi, i, h)),
        out_shape=jax.ShapeDtypeStruct((b, t, D_HEADS * D_DV), F32),
        compiler_params=_cp(('parallel', 'parallel', 'parallel')),
        name='mla_attn',
    )(q, k, v)


def _merge_kernel(ya_ref, yb_ref, yc_ref, yd_ref, g0_ref, g1_ref, g2_ref, g3_ref, x_ref, modb_ref, modc_ref,
                  wb_ref, wo_ref, o_ref, *, tm, n_ctx):
    acc = None
    for r, (y_ref, g_ref) in enumerate(((ya_ref, g0_ref), (yb_ref, g1_ref), (yc_ref, g2_ref), (yd_ref, g3_ref))):
        term = _sigmoid(g_ref[0]) * _dot(y_ref[0].astype(BF16), wb_ref[r])
        acc = term if acc is None else acc + term
    z = _dot(acc.astype(BF16), wo_ref[...])
    gate = _res_gate(modb_ref, modc_ref, pl.program_id(1) * tm, tm, n_ctx, 2)
    o_ref[0] = x_ref[0] + gate * z


def _merge_call(ys, p, x, modb, modc, wb, wo, n_ctx):
    b, t, d = x.shape
    tm = _pick(t, (256, 128))
    y_spec = pl.BlockSpec((1, tm, BRANCH_W), lambda bi, i: (bi, i, 0))
    g_specs = [pl.BlockSpec((1, tm, d), lambda bi, i, r=r: (bi, i, BLK['gates'] // 8 + r)) for r in range(N_BRANCH)]
    x_spec = pl.BlockSpec((1, tm, d), lambda bi, i: (bi, i, 0))
    return pl.pallas_call(
        functools.partial(_merge_kernel, tm=tm, n_ctx=n_ctx),
        grid=(b, t // tm),
        in_specs=[y_spec] * 4 + g_specs + [
            x_spec, pl.BlockSpec((1, 6, d), lambda bi, i: (bi, 0, 0)), pl.BlockSpec((6, d), lambda bi, i: (0, 0)),
            pl.BlockSpec((N_BRANCH, BRANCH_W, d), lambda bi, i: (0, 0, 0)),
            pl.BlockSpec((d, d), lambda bi, i: (0, 0))],
        out_specs=x_spec,
        out_shape=jax.ShapeDtypeStruct((b, t, d), F32),
        compiler_params=_cp(('parallel', 'parallel')),
        name='merge',
    )(*ys, p, p, p, p, x, modb, modc, wb, wo)


def _ffn_kernel(x_ref, modb_ref, modc_ref, g_ref, w1_ref, w2_ref, o_ref, h_ref, acc_ref, *, tm, n_ctx):
    j = pl.program_id(2)
    row0 = pl.program_id(1) * tm

    @pl.when(j == 0)
    def _():
        h_ref[...] = _mod_norm(x_ref, g_ref, modb_ref, modc_ref, row0, tm, n_ctx, 3, 4).astype(BF16)

    u = jnp.square(jnp.maximum(_dot(h_ref[...], w1_ref[...]), 0.0))
    part = _dot(u.astype(BF16), w2_ref[...])

    @pl.when(j == 0)
    def _():
        acc_ref[...] = part

    @pl.when(j > 0)
    def _():
        acc_ref[...] = acc_ref[...] + part

    @pl.when(j == pl.num_programs(2) - 1)
    def _():
        o_ref[0] = x_ref[0] + _res_gate(modb_ref, modc_ref, row0, tm, n_ctx, 5) * acc_ref[...]


def _ffn_call(x, modb, modc, g, w1, w2, n_ctx):
    b, t, d = x.shape
    tm = _pick(t, (768, 512, 384, 256, 128))
    tf = 1024
    x_spec = pl.BlockSpec((1, tm, d), lambda bi, i, j: (bi, i, 0))
    return pl.pallas_call(
        functools.partial(_ffn_kernel, tm=tm, n_ctx=n_ctx),
        grid=(b, t // tm, D_FF // tf),
        in_specs=[x_spec, pl.BlockSpec((1, 6, d), lambda bi, i, j: (bi, 0, 0)),
                  pl.BlockSpec((6, d), lambda bi, i, j: (0, 0)), pl.BlockSpec((1, d), lambda bi, i, j: (0, 0)),
                  pl.BlockSpec((d, tf), lambda bi, i, j: (0, j)), pl.BlockSpec((tf, d), lambda bi, i, j: (j, 0))],
        out_specs=x_spec,
        out_shape=jax.ShapeDtypeStruct((b, t, d), F32),
        scratch_shapes=[pltpu.VMEM((tm, d), BF16), pltpu.VMEM((tm, d), F32)],
        compiler_params=_cp(('parallel', 'parallel', 'arbitrary')),
        name='ffn',
    )(x, modb, modc, g, w1, w2)


def _final_kernel(x_ref, g_ref, o_ref):
    o_ref[0] = _rms(x_ref[0], g_ref[...])


def _final_call(x, g, n_ctx):
    b, t, d = x.shape
    tm = _pick(n_ctx, (256, 128))
    off = n_ctx // tm
    return pl.pallas_call(
        _final_kernel,
        grid=(b, (t - n_ctx) // tm),
        in_specs=[pl.BlockSpec((1, tm, d), lambda bi, i: (bi, i + off, 0)), pl.BlockSpec((1, d), lambda bi, i: (0, 0))],
        out_specs=pl.BlockSpec((1, tm, d), lambda bi, i: (bi, i, 0)),
        out_shape=jax.ShapeDtypeStruct((b, t - n_ctx, d), F32),
        compiler_params=_cp(('parallel', 'parallel')),
        name='final_norm',
    )(x, g)


def _pack_w_in(w_in):
    depth, d, _ = w_in.shape
    ref = lambda n: w_in[:, :, _REF_OFF[n][0]:_REF_OFF[n][0] + _REF_OFF[n][1]]
    zeros = lambda w: jnp.zeros((depth, d, w), w_in.dtype)
    gates = ref('b_gates').reshape(depth, d, 4, B_HEADS)
    pair_gates = [gates[:, :, :, 2 * hp:2 * hp + 2].reshape(depth, d, 8) for hp in range(2)]
    misc0 = jnp.concatenate([ref('d_krope'), pair_gates[0], zeros(LANES - 72)], axis=-1)
    misc1 = jnp.concatenate([zeros(64), pair_gates[1], zeros(LANES - 72)], axis=-1)
    parts = [ref('gates'), ref('a_i'), ref('a_f_fwd'), ref('a_f_bwd'), ref('a_q'), ref('a_g'), ref('b_v'),
             ref('b_o'), ref('c_q'), ref('b_k'), ref('b_q'), ref('d_cq'), ref('c_k'), ref('c_v'), ref('d_ckv'),
             misc0, misc1]
    packed = jnp.concatenate(parts, axis=-1)
    assert packed.shape[-1] == P_WIDTH
    return packed.astype(BF16)


def _rope_tables(n_lat, n_ctx):
    rows = n_lat // GRID_W
    row = jnp.repeat(jnp.arange(rows, dtype=jnp.int32), GRID_W)
    col = jnp.broadcast_to(jnp.arange(GRID_W, dtype=jnp.int32), (rows, GRID_W)).reshape(-1)
    quarter = C_DH // 4
    inv_freq = ROPE_THETA ** (-jnp.arange(quarter, dtype=F32) / quarter)
    ang_r = row.astype(F32)[:, None] * inv_freq
    ang_c = col.astype(F32)[:, None] * inv_freq
    cr, sr, cc, sc = jnp.cos(ang_r), jnp.sin(ang_r), jnp.cos(ang_c), jnp.sin(ang_c)
    cos = jnp.concatenate([cr, cr, cc, cc], axis=-1)
    sin = jnp.concatenate([-sr, sr, -sc, sc], axis=-1)
    cos = jnp.concatenate([jnp.ones((n_ctx, C_DH), F32), cos], axis=0)
    sin = jnp.concatenate([jnp.zeros((n_ctx, C_DH), F32), sin], axis=0)
    return jnp.tile(cos, (1, 2)), jnp.tile(sin, (1, 2))


def kernel(x, c, ctx, c_ctx, w_ada, b_ada, g_norm1, g_norm2, w_in, b_mlstm_gates, hgrn_lb_logits,
           hgrn_norm_g, mlstm_conv_w, mlstm_norm_g, gqa_q_norm_g, gqa_k_norm_g, mla_q_norm_g,
           mla_kv_norm_g, w_mla_uq, w_mla_uk, w_mla_uv, w_branch, w_out, w_ff1, w_ff2, g_final):
    bsz, n_lat, d = x.shape
    n_ctx = ctx.shape[1]
    depth = w_ada.shape[0]
    assert d == D_MODEL and n_lat % GRID_W == 0 and n_ctx % 128 == 0 and n_lat % 128 == 0

    w_in_p = _pack_w_in(w_in)
    w_ada_b = w_ada.astype(BF16)
    wb_b, wo_b = w_branch.astype(BF16), w_out.astype(BF16)
    w1_b, w2_b = w_ff1.astype(BF16), w_ff2.astype(BF16)
    wuk_b, wuv_b = w_mla_uk.astype(BF16), w_mla_uv.astype(BF16)
    uq = w_mla_uq.reshape(depth, D_Q_LORA, D_HEADS, D_NOPE + D_ROPE)
    wqn_b = uq[..., :D_NOPE].reshape(depth, D_Q_LORA, D_HEADS * D_NOPE).astype(BF16)
    wqr_b = uq[..., D_NOPE:].reshape(depth, D_Q_LORA, D_HEADS * D_ROPE).astype(BF16)
    cos_t, sin_t = _rope_tables(n_lat, n_ctx)
    half = lax.broadcasted_iota(jnp.int32, (LANES, LANES), 0) // C_DH == lax.broadcasted_iota(
        jnp.int32, (LANES, LANES), 1) // C_DH
    bd = jnp.where(half, 1.0 / C_DH, 0.0).astype(BF16)
    lb = jnp.cumsum(jax.nn.softmax(hgrn_lb_logits.astype(F32), axis=0), axis=0)
    lb = lb - lb[0]
    llb, l1m = jnp.log(lb), jnp.log1p(-lb)
    gb = b_mlstm_gates.reshape(depth, 4, B_HEADS)
    gate_bias = jnp.zeros((depth, 2, 1, LANES), F32)
    for hp in range(2):
        gate_bias = gate_bias.at[:, hp, 0, GATE_LANE0:GATE_LANE0 + 8].set(gb[:, :, 2 * hp:2 * hp + 2].reshape(depth, 8))
    tile2 = lambda g: jnp.tile(g, (1, 2))

    pad = (-(bsz + 1)) % 8
    s_rows = jnp.concatenate([c, c_ctx[None, :], jnp.zeros((pad, d), F32)], axis=0)
    mods = _ada_call(s_rows, w_ada_b, b_ada)

    xa = jnp.concatenate([ctx, x], axis=1)
    for l in range(depth):
        modb = mods[l, :bsz].reshape(bsz, 6, d)
        modc = mods[l, bsz].reshape(6, d)
        p = _inproj_call(xa, modb, modc, g_norm1[l][None, :], w_in_p[l], n_ctx)
        ya = _hgrn_call(p, llb[l], l1m[l], hgrn_norm_g[l][None, :], n_ctx)
        yb = _mlstm_call(p, mlstm_conv_w[l], gate_bias[l], mlstm_norm_g[l][None, :], n_ctx)
        qc, kc, vc = _gqa_prep_call(p, cos_t, sin_t, tile2(gqa_q_norm_g[l][None, :]),
                                    tile2(gqa_k_norm_g[l][None, :]), bd)
        yc = _gqa_attn_call(qc, kc, vc, n_ctx)
        qd, kd, vd = _mla_prep_call(p, cos_t, sin_t, mla_kv_norm_g[l][None, :], mla_q_norm_g[l][None, :],
                                    wuk_b[l], wuv_b[l], wqn_b[l], wqr_b[l])
        yd = _mla_attn_call(qd, kd, vd, n_ctx)
        xa = _merge_call((ya, yb, yc, yd), p, xa, modb, modc, wb_b[l], wo_b[l], n_ctx)
        xa = _ffn_call(xa, modb, modc, g_norm2[l][None, :], w1_b[l], w2_b[l], n_ctx)
    return _final_call(xa, g_final[None, :], n_ctx)
```

```python
import functools

import numpy as np
import jax
import jax.numpy as jnp
from jax import lax
from jax.experimental import pallas as pl
from jax.experimental.pallas import tpu as pltpu

F32 = jnp.float32
BF16 = jnp.bfloat16

D_MODEL = 1024
GRID_W = 64
EPS = 1e-6
NEG = -1e30
ROPE_THETA = 10000.0
CHUNK = 64
LANES = 128

A_HEADS, A_DK, A_DV = 4, 128, 128
B_HEADS, B_DQK, B_DV, B_CONV = 4, 64, 128, 3
C_HEADS, C_KV_HEADS, C_DH = 8, 2, 64
D_HEADS, D_Q_LORA, D_KV_LORA, D_NOPE, D_ROPE, D_DV = 4, 256, 128, 128, 64, 128
N_BRANCH, BRANCH_W = 4, 512
D_FF = 4 * D_MODEL

_REF_COLS = (
    ('a_i', 512), ('a_f_fwd', 512), ('a_f_bwd', 512), ('b_k', 256), ('b_v', 512), ('b_gates', 16),
    ('c_k', 128), ('c_v', 128), ('d_ckv', 128), ('d_krope', 64),
    ('a_q', 512), ('a_g', 512), ('b_q', 256), ('b_o', 512), ('c_q', 512), ('d_cq', 256), ('gates', 4096),
)
_REF_OFF = {}
_o = 0
for _n, _w in _REF_COLS:
    _REF_OFF[_n] = (_o, _w)
    _o += _w

BLK = dict(a_i=0, a_ff=4, a_fb=8, a_q=12, a_g=16, b_v=20, b_o=24, c_q=28, b_k=32, b_q=34,
           d_cq=36, c_k=38, c_v=39, d_ckv=40, misc0=41, misc1=42)
P_BLOCKS = 44
P_WIDTH = P_BLOCKS * LANES
GATE_LANE0 = 64

VMEM_LIMIT = 56 * 1024 * 1024


def _cp(sem, **kw):
    return pltpu.CompilerParams(dimension_semantics=sem, vmem_limit_bytes=VMEM_LIMIT, **kw)


def _pick(n, cands):
    for c in cands:
        if n % c == 0:
            return c
    raise ValueError(f'no tile for {n}')


def _sigmoid(x):
    return 1.0 / (1.0 + jnp.exp(-x))


def _silu(x):
    return x * _sigmoid(x)


def _softplus_neg_abs(x):
    return jnp.log(1.0 + jnp.exp(-jnp.abs(x)))


def _rms(xf, g):
    ms = jnp.mean(xf * xf, axis=-1, keepdims=True)
    return xf * lax.rsqrt(ms + EPS) * g


def _log_sigmoid(x):
    return jnp.minimum(x, 0.0) - _softplus_neg_abs(x)


def _logaddexp(a, b):
    return jnp.maximum(a, b) + _softplus_neg_abs(a - b)


def _dot(a, b):
    return jnp.dot(a, b, preferred_element_type=F32)


def _dot_nt(a, b):
    return lax.dot_general(a, b, (((1,), (1,)), ((), ())), preferred_element_type=F32)


def _dot_tn(a, b):
    return lax.dot_general(a, b, (((0,), (0,)), ((), ())), preferred_element_type=F32)


def _mod_norm(x_ref, g_ref, modb_ref, modc_ref, row0, tm, n_ctx, k_shift, k_scale):
    y = _rms(x_ref[0], g_ref[...])
    row = row0 + lax.broadcasted_iota(jnp.int32, (tm, 1), 0)
    is_ctx = row < n_ctx
    shift = jnp.where(is_ctx, modc_ref[k_shift:k_shift + 1, :], modb_ref[0, k_shift:k_shift + 1, :])
    scale = jnp.where(is_ctx, modc_ref[k_scale:k_scale + 1, :], modb_ref[0, k_scale:k_scale + 1, :])
    return y * (1.0 + scale) + shift


def _res_gate(modb_ref, modc_ref, row0, tm, n_ctx, k):
    row = row0 + lax.broadcasted_iota(jnp.int32, (tm, 1), 0)
    return jnp.where(row < n_ctx, modc_ref[k:k + 1, :], modb_ref[0, k:k + 1, :])


def _ada_kernel(s_ref, w_ref, b_ref, o_ref):
    s = _silu(s_ref[...])
    o_ref[0] = _dot(s.astype(BF16), w_ref[0]) + b_ref[0]


def _ada_call(s_rows, w_ada, b_ada):
    depth, d, d6 = w_ada.shape
    r = s_rows.shape[0]
    return pl.pallas_call(
        _ada_kernel,
        grid=(depth, d6 // d),
        in_specs=[pl.BlockSpec((r, d), lambda l, j: (0, 0)),
                  pl.BlockSpec((1, d, d), lambda l, j: (l, 0, j)),
                  pl.BlockSpec((1, 1, d), lambda l, j: (l, 0, j))],
        out_specs=pl.BlockSpec((1, r, d), lambda l, j: (l, 0, j)),
        out_shape=jax.ShapeDtypeStruct((depth, r, d6), F32),
        compiler_params=_cp(('parallel', 'parallel')),
        name='ada',
    )(s_rows, w_ada, b_ada.reshape(depth, 1, d6))


def _inproj_kernel(x_ref, modb_ref, modc_ref, g_ref, w_ref, o_ref, h_ref, *, tm, n_ctx):
    @pl.when(pl.program_id(2) == 0)
    def _():
        h = _mod_norm(x_ref, g_ref, modb_ref, modc_ref, pl.program_id(1) * tm, tm, n_ctx, 0, 1)
        h_ref[...] = h.astype(BF16)

    o_ref[0] = _dot(h_ref[...], w_ref[...])


def _inproj_call(x, modb, modc, g, w, n_ctx):
    b, t, d = x.shape
    tm = _pick(t, (768, 512, 384, 256, 128))
    tn = _pick(P_WIDTH, (1408, 2816, 704))
    return pl.pallas_call(
        functools.partial(_inproj_kernel, tm=tm, n_ctx=n_ctx),
        grid=(b, t // tm, P_WIDTH // tn),
        in_specs=[pl.BlockSpec((1, tm, d), lambda bi, i, j: (bi, i, 0)),
                  pl.BlockSpec((1, 6, d), lambda bi, i, j: (bi, 0, 0)),
                  pl.BlockSpec((6, d), lambda bi, i, j: (0, 0)),
                  pl.BlockSpec((1, d), lambda bi, i, j: (0, 0)),
                  pl.BlockSpec((d, tn), lambda bi, i, j: (0, j))],
        out_specs=pl.BlockSpec((1, tm, tn), lambda bi, i, j: (bi, i, j)),
        out_shape=jax.ShapeDtypeStruct((b, t, P_WIDTH), F32),
        scratch_shapes=[pltpu.VMEM((tm, d), BF16)],
        compiler_params=_cp(('parallel', 'parallel', 'arbitrary')),
        name='inproj',
    )(x, modb, modc, g, w)


def _chunk_of_step(step, n_chunks, n_ctx_chunks, rev):
    if not rev:
        return step
    return jnp.where(step < n_ctx_chunks, n_ctx_chunks - 1 - step, n_chunks + n_ctx_chunks - 1 - step)


def _order_mask(n, rev):
    ti = lax.broadcasted_iota(jnp.int32, (n, n), 0)
    si = lax.broadcasted_iota(jnp.int32, (n, n), 1)
    return (si >= ti) if rev else (si <= ti)


_BLOCK_W = (2, 4, 8, 16, 32, 64)
_N_LEVELS = 7
A_HPS = 2


def _hgrn_tables():
    ti, si = np.meshgrid(np.arange(CHUNK), np.arange(CHUNK), indexing='ij')
    mats = [np.tile((ti // w == si // w) & (si <= ti), (1, 4)) for w in _BLOCK_W]
    lvl = np.full((2, CHUNK, CHUNK), -1, np.int32)
    for d in range(2):
        lvl[d][ti == si] = 0
        for i, w in enumerate((1, 2, 4, 8, 16, 32)):
            same = ti // (2 * w) == si // (2 * w)
            t_late = (ti // w) % 2 == (1 - d)
            s_early = (si // w) % 2 == d
            lvl[d][same & t_late & s_early] = i + 1
    return np.concatenate(mats, axis=0).astype(np.float32), lvl


def _block_total(p, w):
    n = p.shape[0]
    if w >= 8:
        parts = [jnp.broadcast_to(p[e - 1:e, :], (w, p.shape[1])) for e in range(w, n + 1, w)]
        return parts[0] if len(parts) == 1 else jnp.concatenate(parts, axis=0)
    x = p.reshape(n // 8, 8, p.shape[1])
    sub = lax.broadcasted_iota(jnp.int32, (1, 8, 1), 1)
    out = jnp.broadcast_to(x[:, 7:8, :], x.shape)
    for e in range(8 - w - 1, -1, -w):
        out = jnp.where(sub <= e, jnp.broadcast_to(x[:, e:e + 1, :], x.shape), out)
    return out.reshape(n, p.shape[1])


def _hgrn_kernel(v_ref, ff_ref, fb_ref, q_ref, g_ref, llb_ref, l1m_ref, ng_ref, bs_ref, lvl_ref, o_ref,
                 accf_ref, accb_ref, qs_ref, *, n_chunks, n_ctx_chunks):
    n = CHUNK
    nw = len(_BLOCK_W)
    f_refs, acc_refs = (ff_ref, fb_ref), (accf_ref, accb_ref)
    streams = [(hh, d) for hh in range(A_HPS) for d in range(2)]
    lanes = lambda hh: slice(hh * A_DK, (hh + 1) * A_DK)

    def body(step, carry):
        rows = []
        for d in range(2):
            c = _chunk_of_step(step, n_chunks, n_ctx_chunks, d == 1)
            rows.append(pl.ds(pl.multiple_of(c * n, n), n))
        lf, kk, q, v, sums = {}, {}, {}, {}, {}
        for hh, d in streams:
            fpre = f_refs[d][0, rows[d], lanes(hh)]
            l1m = l1m_ref[d:d + 1, lanes(hh)]
            softplus = _softplus_neg_abs(fpre)
            lf[hh, d] = _logaddexp(llb_ref[d:d + 1, lanes(hh)], l1m + (jnp.minimum(fpre, 0.0) - softplus))
            kk[hh, d] = jnp.exp(l1m + (jnp.minimum(-fpre, 0.0) - softplus))
            q[hh, d] = qs_ref[rows[d], lanes(hh)]
            v[hh, d] = v_ref[0, rows[d], lanes(hh)].astype(BF16)
        for hh in range(A_HPS):
            parts = []
            for d in range(2):
                hi = lf[hh, d].astype(BF16)
                lo = (lf[hh, d] - hi.astype(F32)).astype(BF16)
                zero = jnp.zeros_like(hi)
                parts += [jnp.concatenate([hi, zero] if d == 0 else [zero, hi], axis=1),
                          jnp.concatenate([lo, zero] if d == 0 else [zero, lo], axis=1)]
            sums_fb = _dot(bs_ref[...], jnp.concatenate(parts, axis=0))
            sums[hh, 0], sums[hh, 1] = sums_fb[:, 0:A_DK], sums_fb[:, A_DK:2 * A_DK]

        a_lv, o_inter, st_new = {}, {}, []
        for si, (hh, d) in enumerate(streams):
            key = (hh, d)
            pre = lambda i, key=key: sums[key][i * n:(i + 1) * n]
            rest = lambda i, pre=pre: _block_total(pre(i), _BLOCK_W[i]) - pre(i)
            if d == 0:
                eq, ek = pre, rest
            else:
                eq = lambda i, key=key, rest=rest: rest(i) + lf[key]
                ek = lambda i, key=key, pre=pre: pre(i) - lf[key]
            kb = kk[key].astype(BF16)
            q01 = jnp.concatenate([q[key], q[key] * jnp.exp(lf[key])], axis=0).astype(BF16)
            a01 = _dot_nt(q01, kb)
            mats = [a01[0:n], a01[n:2 * n]]
            for i in range(nw - 1):
                ql = (q[key] * jnp.exp(eq(i))).astype(BF16)
                kl = (kk[key] * jnp.exp(ek(i))).astype(BF16)
                mats.append(_dot_nt(ql, kl))
            a_lv[key] = mats
            st = carry[si]
            o_inter[key] = _dot_nt((q[key] * jnp.exp(eq(nw - 1))).astype(BF16), st.astype(BF16))
            kdec = (kk[key] * jnp.exp(ek(nw - 1))).astype(BF16)
            total = sums[key][(nw - 1) * n + n - 1:(nw - 1) * n + n]
            st_new.append(jnp.exp(total) * st + _dot_tn(v[key], kdec))

        o_intra = {}
        for key in streams:
            lvl = lvl_ref[key[1]]
            a = jnp.zeros((n, n), F32)
            for i in range(_N_LEVELS):
                a = jnp.where(lvl == i, a_lv[key][i], a)
            o_intra[key] = _dot(a.astype(BF16), v[key])
        for hh, d in streams:
            acc_refs[d][rows[d], lanes(hh)] = o_inter[hh, d] + o_intra[hh, d]
        return tuple(st_new)

    qs_ref[...] = _silu(q_ref[0])
    zero = jnp.zeros((A_DV, A_DK), F32)
    lax.fori_loop(0, n_chunks, body, (zero,) * len(streams), unroll=4 if n_chunks % 4 == 0 else 2)
    for hh in range(A_HPS):
        o = accf_ref[:, lanes(hh)] + accb_ref[:, lanes(hh)]
        o_ref[0, :, lanes(hh)] = (_rms(o, ng_ref[...]) * _silu(g_ref[0, :, lanes(hh)])).astype(o_ref.dtype)


def _hgrn_call(p, llb, l1m, ng, n_ctx):
    b, t, _ = p.shape
    bs_np, lvl_np = _hgrn_tables()
    bs, lvl = jnp.asarray(bs_np, BF16), jnp.asarray(lvl_np)
    wid = A_HPS * A_DK
    col = lambda base: pl.BlockSpec((1, t, wid), lambda bi, h: (bi, 0, base // A_HPS + h))
    par = pl.BlockSpec((2, wid), lambda bi, h: (0, h))
    return pl.pallas_call(
        functools.partial(_hgrn_kernel, n_chunks=t // CHUNK, n_ctx_chunks=n_ctx // CHUNK),
        grid=(b, A_HEADS // A_HPS),
        in_specs=[col(BLK['a_i']), col(BLK['a_ff']), col(BLK['a_fb']), col(BLK['a_q']), col(BLK['a_g']),
                  par, par, pl.BlockSpec((1, A_DV), lambda bi, h: (0, 0)),
                  pl.BlockSpec(bs.shape, lambda bi, h: (0, 0)), pl.BlockSpec(lvl.shape, lambda bi, h: (0, 0, 0))],
        out_specs=pl.BlockSpec((1, t, wid), lambda bi, h: (bi, 0, h)),
        out_shape=jax.ShapeDtypeStruct((b, t, A_HEADS * A_DV), BF16),
        scratch_shapes=[pltpu.VMEM((t, wid), F32), pltpu.VMEM((t, wid), F32), pltpu.VMEM((t, wid), F32)],
        compiler_params=_cp(('parallel', 'parallel')),
        name='hgrn2',
    )(p, p, p, p, p, llb, l1m, ng, bs, lvl)


def _scan_max(x, rev):
    n = x.shape[0]
    row = lax.broadcasted_iota(jnp.int32, (n, 1), 0)
    k = 1
    while k < n:
        if rev:
            shifted = jnp.where(row < n - k, pltpu.roll(x, n - k, 0), NEG)
        else:
            shifted = jnp.where(row >= k, pltpu.roll(x, k, 0), NEG)
        x = jnp.maximum(x, shifted)
        k *= 2
    return x


def _lane_bcast(x, lane, width):
    return jnp.broadcast_to(x[:, lane:lane + 1], (x.shape[0], width))


def _mlstm_kernel(kp_ref, qp_ref, v_ref, og_ref, misc_ref, cw_ref, gb_ref, ng_ref, tri_ref, o_ref,
                  qs_ref, ks_ref, accf_ref, accb_ref, c_ref, *, t, n_ctx):
    n = CHUNK
    n_chunks, n_ctx_chunks = t // n, n_ctx // n
    row = lax.broadcasted_iota(jnp.int32, (t, 1), 0)
    no_prev = (row == 0) | (row == n_ctx)
    no_next = (row == n_ctx - 1) | (row == t - 1)

    def conv_silu(x, w):
        xm = jnp.where(no_prev, 0.0, pltpu.roll(x, 1, 0))
        xp = jnp.where(no_next, 0.0, pltpu.roll(x, t - 1, 0))
        return _silu(w[0:1, :] * xm + w[1:2, :] * x + w[2:3, :] * xp)

    qs_ref[...] = conv_silu(qp_ref[0], cw_ref[0])
    ks_ref[...] = conv_silu(kp_ref[0], cw_ref[1]) * (B_DQK ** -0.5)
    c_ref[...] = jnp.zeros(c_ref.shape, F32)

    lane = lax.broadcasted_iota(jnp.int32, (1, LANES), 1)
    head_lanes = (lane < B_DQK, lane >= B_DQK)
    row2 = lax.broadcasted_iota(jnp.int32, (2 * B_DQK, 1), 0)
    acc_refs = (accf_ref, accb_ref)
    streams = [(d, j) for d in range(2) for j in range(2)]
    gate_lane = lambda d, j: GATE_LANE0 + 4 * d + j

    def body(step, m_prev):
        rows, g_all, bal_all = [], [], []
        for d in range(2):
            c = _chunk_of_step(step, n_chunks, n_ctx_chunks, d == 1)
            rows.append(pl.ds(pl.multiple_of(c * n, n), n))
            g = misc_ref[0, rows[d], :] + gb_ref[0]
            ls = _log_sigmoid(g)
            hi = ls.astype(BF16)
            lo = (ls - hi.astype(F32)).astype(BF16)
            bc = _dot(tri_ref[d], hi) + _dot(tri_ref[d], lo)
            g_all.append(g)
            bal_all.append(pltpu.roll(bc, LANES - 2, 1))

        qm, kf, qk, qc = {}, {}, {}, {}
        for d, j in streams:
            kf[d] = ks_ref[rows[d], :]
            qm[d, j] = jnp.where(head_lanes[j], qs_ref[rows[d], :], 0.0).astype(BF16)
            qk[d, j] = _dot_nt(qm[d, j], kf[d].astype(BF16))
            qc[d, j] = _dot(qm[d, j], c_ref[d].astype(BF16))

        u_all, ut_all, cu_all, r_all, w_all, em_all, wk_all, dec_all, m_new = [], [], [], [], [], [], [], [], []
        for d in range(2):
            last = 0 if d == 1 else n - 1
            u = g_all[d] - bal_all[d]
            cu = _scan_max(u, d == 1)
            gm = jnp.maximum(m_prev[d], cu)
            w = jnp.exp(m_prev[d] - gm)
            gl = gm[last:last + 1]
            u_all.append(u)
            ut_all.append(jnp.concatenate([u, cu], axis=0).T[:, 0:n])
            cu_all.append(cu)
            r_all.append(jnp.exp(cu - gm))
            w_all.append(w)
            em_all.append(jnp.exp(-bal_all[d] - gm))
            wk_all.append(jnp.exp(u - gl))
            dec_all.append(w[last:last + 1])
            m_new.append(bal_all[d][last:last + 1] + gl)

        un, uk = {}, {}
        for d, j in streams:
            li = gate_lane(d, j)
            expo = ut_all[d][li:li + 1, :] - _lane_bcast(cu_all[d], li, n)
            s0 = (qk[d, j] * jnp.exp(jnp.where(_order_mask(n, d == 1), expo, NEG))).astype(BF16)
            one_col = jnp.broadcast_to(jnp.where(lane == li, 1.0, 0.0), (n, LANES))
            vaug = jnp.concatenate([v_ref[0, rows[d], j * B_DV:(j + 1) * B_DV], one_col], axis=1).astype(BF16)
            un[d, j] = _dot(s0, vaug)
            kw = jnp.where(head_lanes[j], kf[d], 0.0) * _lane_bcast(wk_all[d], li, LANES)
            uk[d, j] = _dot_tn(kw.astype(BF16), vaug)

        for d, j in streams:
            li = gate_lane(d, j)
            num = (_lane_bcast(w_all[d], li, B_DV) * qc[d, j][:, 0:B_DV]
                   + _lane_bcast(r_all[d], li, B_DV) * un[d, j][:, 0:B_DV])
            den = w_all[d] * qc[d, j][:, B_DV:] + r_all[d] * un[d, j][:, B_DV:]
            z = 1.0 / jnp.maximum(jnp.abs(den), em_all[d])
            acc_refs[d][rows[d], j * B_DV:(j + 1) * B_DV] = num * _lane_bcast(z, li, B_DV)
        for d in range(2):
            l0, l1 = gate_lane(d, 0), gate_lane(d, 1)
            dec = jnp.where(row2 < B_DQK, dec_all[d][:, l0:l0 + 1], dec_all[d][:, l1:l1 + 1])
            c_ref[d] = dec * c_ref[d] + uk[d, 0] + uk[d, 1]
        return tuple(m_new)

    m0 = jnp.full((1, LANES), NEG, F32)
    lax.fori_loop(0, n_chunks, body, (m0, m0), unroll=2)

    for j in range(2):
        cols = slice(j * B_DV, (j + 1) * B_DV)
        h = accf_ref[:, cols] + accb_ref[:, cols]
        o_ref[0, :, cols] = (_rms(h, ng_ref[...]) * _sigmoid(og_ref[0, :, cols])).astype(o_ref.dtype)


def _mlstm_call(p, conv_w, gate_bias, ng, n_ctx):
    b, t, _ = p.shape
    pairs = B_HEADS // 2
    tri = jnp.stack([_order_mask(CHUNK, False), _order_mask(CHUNK, True)]).astype(BF16)
    return pl.pallas_call(
        functools.partial(_mlstm_kernel, t=t, n_ctx=n_ctx),
        grid=(b, pairs),
        in_specs=[pl.BlockSpec((1, t, LANES), lambda bi, hp: (bi, 0, BLK['b_k'] + hp)),
                  pl.BlockSpec((1, t, LANES), lambda bi, hp: (bi, 0, BLK['b_q'] + hp)),
                  pl.BlockSpec((1, t, 2 * B_DV), lambda bi, hp: (bi, 0, BLK['b_v'] // 2 + hp)),
                  pl.BlockSpec((1, t, 2 * B_DV), lambda bi, hp: (bi, 0, BLK['b_o'] // 2 + hp)),
                  pl.BlockSpec((1, t, LANES), lambda bi, hp: (bi, 0, BLK['misc0'] + hp)),
                  pl.BlockSpec((2, B_CONV, LANES), lambda bi, hp: (0, 0, hp)),
                  pl.BlockSpec((1, 1, LANES), lambda bi, hp: (hp, 0, 0)),
                  pl.BlockSpec((1, LANES), lambda bi, hp: (0, 0)),
                  pl.BlockSpec((2, CHUNK, CHUNK), lambda bi, hp: (0, 0, 0))],
        out_specs=pl.BlockSpec((1, t, 2 * B_DV), lambda bi, hp: (bi, 0, hp)),
        out_shape=jax.ShapeDtypeStruct((b, t, B_HEADS * B_DV), BF16),
        scratch_shapes=[pltpu.VMEM((t, LANES), F32), pltpu.VMEM((t, LANES), F32),
                        pltpu.VMEM((t, 2 * B_DV), F32), pltpu.VMEM((t, 2 * B_DV), F32),
                        pltpu.VMEM((2, 2 * B_DQK, 2 * B_DV), F32)],
        compiler_params=_cp(('parallel', 'parallel')),
        name='mlstm',
    )(p, p, p, p, p, conv_w, gate_bias, ng, tri)


def _rope_tile(x, cos, sin_signed):
    lane = lax.broadcasted_iota(jnp.int32, (1, LANES), 1)
    first = (lane // (C_DH // 4)) % 2 == 0
    partner = jnp.where(first, pltpu.roll(x, LANES - C_DH // 4, 1), pltpu.roll(x, C_DH // 4, 1))
    return x * cos + partner * sin_signed


def _head_norm_tile(x, g, bd):
    sq = x * x
    hi = sq.astype(BF16)
    lo = (sq - hi.astype(F32)).astype(BF16)
    ms = _dot(hi, bd) + _dot(lo, bd)
    return x * lax.rsqrt(ms + EPS) * g


def _half_to(x, src_hi, dst_hi):
    lane = lax.broadcasted_iota(jnp.int32, (1, LANES), 1)
    if src_hi != dst_hi:
        x = pltpu.roll(x, LANES // 2, 1)
    keep = (lane >= LANES // 2) if dst_hi else (lane < LANES // 2)
    return jnp.where(keep, x, 0.0)


def _gqa_prep_kernel(cq_ref, ck_ref, cv_ref, cos_ref, sin_ref, qg_ref, kg_ref, bd_ref,
                     q_ref, k_ref, v_ref):
    cos, sin, bd = cos_ref[...], sin_ref[...], bd_ref[...]
    k = _head_norm_tile(ck_ref[0], kg_ref[...], bd)
    k_ref[0] = _rope_tile(k, cos, sin).astype(BF16)
    v_ref[0] = cv_ref[0].astype(BF16)
    grp = C_HEADS // C_KV_HEADS
    for mt in range(C_HEADS // 2):
        x = _head_norm_tile(cq_ref[0, :, mt * LANES:(mt + 1) * LANES], qg_ref[...], bd)
        x = _rope_tile(x, cos, sin) * (C_DH ** -0.5)
        for half in range(2):
            head = 2 * mt + half
            q_ref[0, head] = _half_to(x, half == 1, head // grp == 1).astype(BF16)


def _gqa_prep_call(p, cos, sin, qg, kg, bd):
    b, t, _ = p.shape
    tm = _pick(t, (256, 128))
    tok = lambda blk, w: pl.BlockSpec((1, tm, w), lambda bi, i: (bi, i, blk))
    tab = pl.BlockSpec((tm, LANES), lambda bi, i: (i, 0))
    vec = pl.BlockSpec((1, LANES), lambda bi, i: (0, 0))
    kv_spec = pl.BlockSpec((1, tm, LANES), lambda bi, i: (bi, i, 0))
    return pl.pallas_call(
        _gqa_prep_kernel,
        grid=(b, t // tm),
        in_specs=[tok(BLK['c_q'] // 4, 4 * LANES), tok(BLK['c_k'], LANES), tok(BLK['c_v'], LANES),
                  tab, tab, vec, vec, pl.BlockSpec((LANES, LANES), lambda bi, i: (0, 0))],
        out_specs=[pl.BlockSpec((1, C_HEADS, tm, LANES), lambda bi, i: (bi, 0, i, 0)), kv_spec, kv_spec],
        out_shape=[jax.ShapeDtypeStruct((b, C_HEADS, t, LANES), BF16),
                   jax.ShapeDtypeStruct((b, t, LANES), BF16),
                   jax.ShapeDtypeStruct((b, t, LANES), BF16)],
        compiler_params=_cp(('parallel', 'parallel')),
        name='gqa_prep',
    )(p, p, p, cos, sin, qg, kg, bd)


def _softmax_pv(q, k, v, scale):
    s = _dot_nt(q, k)
    if scale is not None:
        s = s * scale
    m = jnp.max(s, axis=-1, keepdims=True)
    e = jnp.exp(s - m)
    l = jnp.sum(e, axis=-1, keepdims=True)
    return _dot(e.astype(BF16), v) / l


def _gqa_attn_kernel(q_ref, k_ref, v_ref, o_ref, *, tq, t, n_ctx):
    grp = C_HEADS // C_KV_HEADS
    lane = lax.broadcasted_iota(jnp.int32, (1, LANES), 1)

    def attend(nk):
        k, v = k_ref[0, 0:nk, :], v_ref[0, 0:nk, :]
        for g in range(C_KV_HEADS):
            q = q_ref[0, g * grp:(g + 1) * grp].reshape(grp * tq, LANES)
            o = _softmax_pv(q, k, v, None)
            for pair in range(grp // 2):
                o_even = o[(2 * pair) * tq:(2 * pair + 1) * tq]
                o_odd = o[(2 * pair + 1) * tq:(2 * pair + 2) * tq]
                lo = o_even if g == 0 else pltpu.roll(o_even, LANES // 2, 1)
                hi = pltpu.roll(o_odd, LANES // 2, 1) if g == 0 else o_odd
                col = (g * grp // 2 + pair) * LANES
                o_ref[0, :, col:col + LANES] = jnp.where(lane < LANES // 2, lo, hi).astype(o_ref.dtype)

    is_ctx = pl.program_id(1) * tq < n_ctx
    pl.when(is_ctx)(lambda: attend(n_ctx))
    pl.when(jnp.logical_not(is_ctx))(lambda: attend(t))


def _gqa_attn_call(q, k, v, n_ctx):
    b, _, t, _ = q.shape
    tq = _pick(n_ctx, (128,))
    kv_spec = pl.BlockSpec((1, t, LANES), lambda bi, i: (bi, 0, 0))
    return pl.pallas_call(
        functools.partial(_gqa_attn_kernel, tq=tq, t=t, n_ctx=n_ctx),
        grid=(b, t // tq),
        in_specs=[pl.BlockSpec((1, C_HEADS, tq, LANES), lambda bi, i: (bi, 0, i, 0)), kv_spec, kv_spec],
        out_specs=pl.BlockSpec((1, tq, C_HEADS * C_DH), lambda bi, i: (bi, i, 0)),
        out_shape=jax.ShapeDtypeStruct((b, t, C_HEADS * C_DH), BF16),
        compiler_params=_cp(('parallel', 'parallel')),
        name='gqa_attn',
    )(q, k, v)


def _mla_prep_kernel(ckv_ref, misc_ref, cq_ref, cos_ref, sin_ref, kvg_ref, qg_ref,
                     wuk_ref, wuv_ref, wqn_ref, wqr_ref, q_ref, k_ref, v_ref):
    cos, sin = cos_ref[...], sin_ref[...]
    ckv = _rms(ckv_ref[0], kvg_ref[...]).astype(BF16)
    k_nope = _dot(ckv, wuk_ref[...])
    v_ref[0] = _dot(ckv, wuv_ref[...]).astype(BF16)
    k_rope = _half_to(_rope_tile(misc_ref[0], cos, sin), False, False).astype(BF16)
    cq = _rms(cq_ref[0], qg_ref[...]).astype(BF16)
    q_nope = _dot(cq, wqn_ref[...])
    q_rope = _dot(cq, wqr_ref[...])
    for mt in range(D_HEADS // 2):
        qr = _rope_tile(q_rope[:, mt * LANES:(mt + 1) * LANES], cos, sin)
        for half in range(2):
            h = 2 * mt + half
            q_ref[0, h, :, 0:D_NOPE] = q_nope[:, h * D_NOPE:(h + 1) * D_NOPE].astype(BF16)
            q_ref[0, h, :, D_NOPE:2 * D_NOPE] = _half_to(qr, half == 1, False).astype(BF16)
            k_ref[0, h, :, 0:D_NOPE] = k_nope[:, h * D_NOPE:(h + 1) * D_NOPE].astype(BF16)
            k_ref[0, h, :, D_NOPE:2 * D_NOPE] = k_rope


def _mla_prep_call(p, cos, sin, kvg, qg, wuk, wuv, wqn, wqr):
    b, t, _ = p.shape
    tm = _pick(t, (256, 128))
    tok = lambda blk, w: pl.BlockSpec((1, tm, w), lambda bi, i: (bi, i, blk))
    tab = pl.BlockSpec((tm, LANES), lambda bi, i: (i, 0))
    full = lambda a: pl.BlockSpec(a.shape, lambda bi, i: (0,) * a.ndim)
    hd = 2 * D_NOPE
    qk_spec = pl.BlockSpec((1, D_HEADS, tm, hd), lambda bi, i: (bi, 0, i, 0))
    return pl.pallas_call(
        _mla_prep_kernel,
        grid=(b, t // tm),
        in_specs=[tok(BLK['d_ckv'], LANES), tok(BLK['misc0'], LANES), tok(BLK['d_cq'] // 2, 2 * LANES),
                  tab, tab, full(kvg), full(qg), full(wuk), full(wuv), full(wqn), full(wqr)],
        out_specs=[qk_spec, qk_spec, pl.BlockSpec((1, tm, D_HEADS * D_DV), lambda bi, i: (bi, i, 0))],
        out_shape=[jax.ShapeDtypeStruct((b, D_HEADS, t, hd), BF16),
                   jax.ShapeDtypeStruct((b, D_HEADS, t, hd), BF16),
                   jax.ShapeDtypeStruct((b, t, D_HEADS * D_DV), BF16)],
        compiler_params=_cp(('parallel', 'parallel')),
        name='mla_prep',
    )(p, p, p, cos, sin, kvg, qg, wuk, wuv, wqn, wqr)


def _mla_attn_kernel(q_ref, k_ref, v_ref, o_ref, *, tq, t, n_ctx):
    def attend(nk):
        o_ref[0] = _softmax_pv(q_ref[0, 0], k_ref[0, 0, 0:nk, :], v_ref[0, 0:nk, :],
                               (D_NOPE + D_ROPE) ** -0.5).astype(o_ref.dtype)

    is_ctx = pl.program_id(2) * tq < n_ctx
    pl.when(is_ctx)(lambda: attend(n_ctx))
    pl.when(jnp.logical_not(is_ctx))(lambda: attend(t))


def _mla_attn_call(q, k, v, n_ctx):
    b, _, t, hd = q.shape
    tq = _pick(n_ctx, (256, 128))
    return pl.pallas_call(
        functools.partial(_mla_attn_kernel, tq=tq, t=t, n_ctx=n_ctx),
        grid=(b, D_HEADS, t // tq),
        in_specs=[pl.BlockSpec((1, 1, tq, hd), lambda bi, h, i: (bi, h, i, 0)),
                  pl.BlockSpec((1, 1, t, hd), lambda bi, h, i: (bi, h, 0, 0)),
                  pl.BlockSpec((1, t, D_DV), lambda bi, h, i: (bi, 0, h))],
        out_specs=pl.BlockSpec((1, tq, D_DV), lambda bi, h, i: (bi, i, h)),
        out_shape=jax.ShapeDtypeStruct((b, t, D_HEADS * D_DV), BF16),
        compiler_params=_cp(('parallel', 'parallel', 'parallel')),
        name='mla_attn',
    )(q, k, v)


def _merge_kernel(ya_ref, yb_ref, yc_ref, yd_ref, x_ref, modb_ref, modc_ref, g_ref, wg_ref, wb_ref, wo_ref,
                  o_ref, *, tm, n_ctx):
    row0 = pl.program_id(1) * tm
    h = _mod_norm(x_ref, g_ref, modb_ref, modc_ref, row0, tm, n_ctx, 0, 1).astype(BF16)
    acc = None
    for r, y_ref in enumerate((ya_ref, yb_ref, yc_ref, yd_ref)):
        term = _sigmoid(_dot(h, wg_ref[r])) * _dot(y_ref[0], wb_ref[r])
        acc = term if acc is None else acc + term
    z = _dot(acc.astype(BF16), wo_ref[...])
    o_ref[0] = x_ref[0] + _res_gate(modb_ref, modc_ref, row0, tm, n_ctx, 2) * z


def _merge_call(ys, x, modb, modc, g, wg, wb, wo, n_ctx):
    b, t, d = x.shape
    tm = _pick(t, (256, 128))
    y_spec = pl.BlockSpec((1, tm, BRANCH_W), lambda bi, i: (bi, i, 0))
    x_spec = pl.BlockSpec((1, tm, d), lambda bi, i: (bi, i, 0))
    return pl.pallas_call(
        functools.partial(_merge_kernel, tm=tm, n_ctx=n_ctx),
        grid=(b, t // tm),
        in_specs=[y_spec] * 4 + [
            x_spec, pl.BlockSpec((1, 6, d), lambda bi, i: (bi, 0, 0)), pl.BlockSpec((6, d), lambda bi, i: (0, 0)),
            pl.BlockSpec((1, d), lambda bi, i: (0, 0)),
            pl.BlockSpec((N_BRANCH, d, d), lambda bi, i: (0, 0, 0)),
            pl.BlockSpec((N_BRANCH, BRANCH_W, d), lambda bi, i: (0, 0, 0)),
            pl.BlockSpec((d, d), lambda bi, i: (0, 0))],
        out_specs=x_spec,
        out_shape=jax.ShapeDtypeStruct((b, t, d), F32),
        compiler_params=_cp(('parallel', 'parallel')),
        name='merge',
    )(*ys, x, modb, modc, g, wg, wb, wo)


def _ffn_kernel(x_ref, modb_ref, modc_ref, g_ref, w1_ref, w2_ref, o_ref, h_ref, acc_ref, *, tm, n_ctx):
    j = pl.program_id(2)
    row0 = pl.program_id(1) * tm

    @pl.when(j == 0)
    def _():
        h_ref[...] = _mod_norm(x_ref, g_ref, modb_ref, modc_ref, row0, tm, n_ctx, 3, 4).astype(BF16)

    u = jnp.square(jnp.maximum(_dot(h_ref[...], w1_ref[...]), 0.0))
    part = _dot(u.astype(BF16), w2_ref[...])

    @pl.when(j == 0)
    def _():
        acc_ref[...] = part

    @pl.when(j > 0)
    def _():
        acc_ref[...] = acc_ref[...] + part

    @pl.when(j == pl.num_programs(2) - 1)
    def _():
        o_ref[0] = x_ref[0] + _res_gate(modb_ref, modc_ref, row0, tm, n_ctx, 5) * acc_ref[...]


def _ffn_call(x, modb, modc, g, w1, w2, n_ctx):
    b, t, d = x.shape
    tm = _pick(t, (768, 512, 384, 256, 128))
    tf = 1024
    x_spec = pl.BlockSpec((1, tm, d), lambda bi, i, j: (bi, i, 0))
    return pl.pallas_call(
        functools.partial(_ffn_kernel, tm=tm, n_ctx=n_ctx),
        grid=(b, t // tm, D_FF // tf),
        in_specs=[x_spec, pl.BlockSpec((1, 6, d), lambda bi, i, j: (bi, 0, 0)),
                  pl.BlockSpec((6, d), lambda bi, i, j: (0, 0)), pl.BlockSpec((1, d), lambda bi, i, j: (0, 0)),
                  pl.BlockSpec((d, tf), lambda bi, i, j: (0, j)), pl.BlockSpec((tf, d), lambda bi, i, j: (j, 0))],
        out_specs=x_spec,
        out_shape=jax.ShapeDtypeStruct((b, t, d), F32),
        scratch_shapes=[pltpu.VMEM((tm, d), BF16), pltpu.VMEM((tm, d), F32)],
        compiler_params=_cp(('parallel', 'parallel', 'arbitrary')),
        name='ffn',
    )(x, modb, modc, g, w1, w2)


def _final_kernel(x_ref, g_ref, o_ref):
    o_ref[0] = _rms(x_ref[0], g_ref[...])


def _final_call(x, g, n_ctx):
    b, t, d = x.shape
    tm = _pick(n_ctx, (256, 128))
    off = n_ctx // tm
    return pl.pallas_call(
        _final_kernel,
        grid=(b, (t - n_ctx) // tm),
        in_specs=[pl.BlockSpec((1, tm, d), lambda bi, i: (bi, i + off, 0)), pl.BlockSpec((1, d), lambda bi, i: (0, 0))],
        out_specs=pl.BlockSpec((1, tm, d), lambda bi, i: (bi, i, 0)),
        out_shape=jax.ShapeDtypeStruct((b, t - n_ctx, d), F32),
        compiler_params=_cp(('parallel', 'parallel')),
        name='final_norm',
    )(x, g)


def _pack_w_in(w_in):
    depth, d, _ = w_in.shape
    ref = lambda n: w_in[:, :, _REF_OFF[n][0]:_REF_OFF[n][0] + _REF_OFF[n][1]]
    zeros = lambda w: jnp.zeros((depth, d, w), w_in.dtype)
    gates = ref('b_gates').reshape(depth, d, 4, B_HEADS)
    pair_gates = [gates[:, :, :, 2 * hp:2 * hp + 2].reshape(depth, d, 8) for hp in range(2)]
    misc0 = jnp.concatenate([ref('d_krope'), pair_gates[0], zeros(LANES - 72)], axis=-1)
    misc1 = jnp.concatenate([zeros(64), pair_gates[1], zeros(LANES - 72)], axis=-1)
    parts = [ref('a_i'), ref('a_f_fwd'), ref('a_f_bwd'), ref('a_q'), ref('a_g'), ref('b_v'),
             ref('b_o'), ref('c_q'), ref('b_k'), ref('b_q'), ref('d_cq'), ref('c_k'), ref('c_v'), ref('d_ckv'),
             misc0, misc1, zeros(LANES)]
    packed = jnp.concatenate(parts, axis=-1)
    assert packed.shape[-1] == P_WIDTH
    return packed.astype(BF16)


def _rope_tables(n_lat, n_ctx):
    rows = n_lat // GRID_W
    row = jnp.repeat(jnp.arange(rows, dtype=jnp.int32), GRID_W)
    col = jnp.broadcast_to(jnp.arange(GRID_W, dtype=jnp.int32), (rows, GRID_W)).reshape(-1)
    quarter = C_DH // 4
    inv_freq = ROPE_THETA ** (-jnp.arange(quarter, dtype=F32) / quarter)
    ang_r = row.astype(F32)[:, None] * inv_freq
    ang_c = col.astype(F32)[:, None] * inv_freq
    cr, sr, cc, sc = jnp.cos(ang_r), jnp.sin(ang_r), jnp.cos(ang_c), jnp.sin(ang_c)
    cos = jnp.concatenate([cr, cr, cc, cc], axis=-1)
    sin = jnp.concatenate([-sr, sr, -sc, sc], axis=-1)
    cos = jnp.concatenate([jnp.ones((n_ctx, C_DH), F32), cos], axis=0)
    sin = jnp.concatenate([jnp.zeros((n_ctx, C_DH), F32), sin], axis=0)
    return jnp.tile(cos, (1, 2)), jnp.tile(sin, (1, 2))


def kernel(x, c, ctx, c_ctx, w_ada, b_ada, g_norm1, g_norm2, w_in, b_mlstm_gates, hgrn_lb_logits,
           hgrn_norm_g, mlstm_conv_w, mlstm_norm_g, gqa_q_norm_g, gqa_k_norm_g, mla_q_norm_g,
           mla_kv_norm_g, w_mla_uq, w_mla_uk, w_mla_uv, w_branch, w_out, w_ff1, w_ff2, g_final):
    bsz, n_lat, d = x.shape
    n_ctx = ctx.shape[1]
    depth = w_ada.shape[0]
    assert d == D_MODEL and n_lat % GRID_W == 0 and n_ctx % 128 == 0 and n_lat % 128 == 0

    w_in_p = _pack_w_in(w_in)
    w_ada_b = w_ada.astype(BF16)
    wb_b, wo_b = w_branch.astype(BF16), w_out.astype(BF16)
    g0, gw = _REF_OFF['gates']
    wg_b = w_in[:, :, g0:g0 + gw].reshape(depth, d, N_BRANCH, d).transpose(0, 2, 1, 3).astype(BF16)
    w1_b, w2_b = w_ff1.astype(BF16), w_ff2.astype(BF16)
    wuk_b, wuv_b = w_mla_uk.astype(BF16), w_mla_uv.astype(BF16)
    uq = w_mla_uq.reshape(depth, D_Q_LORA, D_HEADS, D_NOPE + D_ROPE)
    wqn_b = uq[..., :D_NOPE].reshape(depth, D_Q_LORA, D_HEADS * D_NOPE).astype(BF16)
    wqr_b = uq[..., D_NOPE:].reshape(depth, D_Q_LORA, D_HEADS * D_ROPE).astype(BF16)
    cos_t, sin_t = _rope_tables(n_lat, n_ctx)
    half = lax.broadcasted_iota(jnp.int32, (LANES, LANES), 0) // C_DH == lax.broadcasted_iota(
        jnp.int32, (LANES, LANES), 1) // C_DH
    bd = jnp.where(half, 1.0 / C_DH, 0.0).astype(BF16)
    lb = jnp.cumsum(jax.nn.softmax(hgrn_lb_logits.astype(F32), axis=0), axis=0)
    lb = lb - lb[0]
    llb, l1m = jnp.log(lb), jnp.log1p(-lb)
    gb = b_mlstm_gates.reshape(depth, 4, B_HEADS)
    gate_bias = jnp.zeros((depth, 2, 1, LANES), F32)
    for hp in range(2):
        gate_bias = gate_bias.at[:, hp, 0, GATE_LANE0:GATE_LANE0 + 8].set(gb[:, :, 2 * hp:2 * hp + 2].reshape(depth, 8))
    tile2 = lambda g: jnp.tile(g, (1, 2))

    pad = (-(bsz + 1)) % 8
    s_rows = jnp.concatenate([c, c_ctx[None, :], jnp.zeros((pad, d), F32)], axis=0)
    mods = _ada_call(s_rows, w_ada_b, b_ada)

    xa = jnp.concatenate([ctx, x], axis=1)
    for l in range(depth):
        modb = mods[l, :bsz].reshape(bsz, 6, d)
        modc = mods[l, bsz].reshape(6, d)
        p = _inproj_call(xa, modb, modc, g_norm1[l][None, :], w_in_p[l], n_ctx)
        ya = _hgrn_call(p, llb[l], l1m[l], hgrn_norm_g[l][None, :], n_ctx)
        yb = _mlstm_call(p, mlstm_conv_w[l], gate_bias[l], mlstm_norm_g[l][None, :], n_ctx)
        qc, kc, vc = _gqa_prep_call(p, cos_t, sin_t, tile2(gqa_q_norm_g[l][None, :]),
                                    tile2(gqa_k_norm_g[l][None, :]), bd)
        yc = _gqa_attn_call(qc, kc, vc, n_ctx)
        qd, kd, vd = _mla_prep_call(p, cos_t, sin_t, mla_kv_norm_g[l][None, :], mla_q_norm_g[l][None, :],
                                    wuk_b[l], wuv_b[l], wqn_b[l], wqr_b[l])
        yd = _mla_attn_call(qd, kd, vd, n_ctx)
        xa = _merge_call((ya, yb, yc, yd), xa, modb, modc, g_norm1[l][None, :], wg_b[l], wb_b[l], wo_b[l], n_ctx)
        xa = _ffn_call(xa, modb, modc, g_norm2[l][None, :], w1_b[l], w2_b[l], n_ctx)
    return _final_call(xa, g_final[None, :], n_ctx)
```

```python
import functools

import numpy as np
import jax
import jax.numpy as jnp
from jax import lax
from jax.experimental import pallas as pl
from jax.experimental.pallas import tpu as pltpu

F32 = jnp.float32
BF16 = jnp.bfloat16

D_MODEL = 1024
GRID_W = 64
EPS = 1e-6
NEG = -1e30
ROPE_THETA = 10000.0
CHUNK = 64
LANES = 128

A_HEADS, A_DK, A_DV = 4, 128, 128
B_HEADS, B_DQK, B_DV, B_CONV = 4, 64, 128, 3
C_HEADS, C_KV_HEADS, C_DH = 8, 2, 64
D_HEADS, D_Q_LORA, D_KV_LORA, D_NOPE, D_ROPE, D_DV = 4, 256, 128, 128, 64, 128
N_BRANCH, BRANCH_W = 4, 512
D_FF = 4 * D_MODEL

_REF_COLS = (
    ('a_i', 512), ('a_f_fwd', 512), ('a_f_bwd', 512), ('b_k', 256), ('b_v', 512), ('b_gates', 16),
    ('c_k', 128), ('c_v', 128), ('d_ckv', 128), ('d_krope', 64),
    ('a_q', 512), ('a_g', 512), ('b_q', 256), ('b_o', 512), ('c_q', 512), ('d_cq', 256), ('gates', 4096),
)
_REF_OFF = {}
_o = 0
for _n, _w in _REF_COLS:
    _REF_OFF[_n] = (_o, _w)
    _o += _w

BLK = dict(a_i=0, a_ff=4, a_fb=8, a_q=12, a_g=16, b_v=20, b_o=24, c_q=28, b_k=32, b_q=34,
           d_cq=36, c_k=38, c_v=39, d_ckv=40, misc0=41, misc1=42)
P_BLOCKS = 44
P_WIDTH = P_BLOCKS * LANES
GATE_LANE0 = 64

VMEM_LIMIT = 56 * 1024 * 1024


def _cp(sem, **kw):
    return pltpu.CompilerParams(dimension_semantics=sem, vmem_limit_bytes=VMEM_LIMIT, **kw)


def _pick(n, cands):
    for c in cands:
        if n % c == 0:
            return c
    raise ValueError(f'no tile for {n}')


def _sigmoid(x):
    return 1.0 / (1.0 + jnp.exp(-x))


def _silu(x):
    return x * _sigmoid(x)


def _softplus_neg_abs(x):
    return jnp.log(1.0 + jnp.exp(-jnp.abs(x)))


def _rms(xf, g):
    ms = jnp.mean(xf * xf, axis=-1, keepdims=True)
    return xf * lax.rsqrt(ms + EPS) * g


def _log_sigmoid(x):
    return jnp.minimum(x, 0.0) - _softplus_neg_abs(x)


def _logaddexp(a, b):
    return jnp.maximum(a, b) + _softplus_neg_abs(a - b)


def _dot(a, b):
    return jnp.dot(a, b, preferred_element_type=F32)


def _dot_nt(a, b):
    return lax.dot_general(a, b, (((1,), (1,)), ((), ())), preferred_element_type=F32)


def _dot_tn(a, b):
    return lax.dot_general(a, b, (((0,), (0,)), ((), ())), preferred_element_type=F32)


def _mod_norm(x_ref, g_ref, modb_ref, modc_ref, row0, tm, n_ctx, k_shift, k_scale):
    y = _rms(x_ref[0], g_ref[...])
    row = row0 + lax.broadcasted_iota(jnp.int32, (tm, 1), 0)
    is_ctx = row < n_ctx
    shift = jnp.where(is_ctx, modc_ref[k_shift:k_shift + 1, :], modb_ref[0, k_shift:k_shift + 1, :])
    scale = jnp.where(is_ctx, modc_ref[k_scale:k_scale + 1, :], modb_ref[0, k_scale:k_scale + 1, :])
    return y * (1.0 + scale) + shift


def _res_gate(modb_ref, modc_ref, row0, tm, n_ctx, k):
    row = row0 + lax.broadcasted_iota(jnp.int32, (tm, 1), 0)
    return jnp.where(row < n_ctx, modc_ref[k:k + 1, :], modb_ref[0, k:k + 1, :])


def _ada_kernel(s_ref, w_ref, b_ref, o_ref):
    s = _silu(s_ref[...])
    o_ref[0] = _dot(s.astype(BF16), w_ref[0]) + b_ref[0]


def _ada_call(s_rows, w_ada, b_ada):
    depth, d, d6 = w_ada.shape
    r = s_rows.shape[0]
    return pl.pallas_call(
        _ada_kernel,
        grid=(depth, d6 // d),
        in_specs=[pl.BlockSpec((r, d), lambda l, j: (0, 0)),
                  pl.BlockSpec((1, d, d), lambda l, j: (l, 0, j)),
                  pl.BlockSpec((1, 1, d), lambda l, j: (l, 0, j))],
        out_specs=pl.BlockSpec((1, r, d), lambda l, j: (l, 0, j)),
        out_shape=jax.ShapeDtypeStruct((depth, r, d6), F32),
        compiler_params=_cp(('parallel', 'parallel')),
        name='ada',
    )(s_rows, w_ada, b_ada.reshape(depth, 1, d6))


def _inproj_kernel(x_ref, modb_ref, modc_ref, g_ref, w_ref, o_ref, h_ref, *, tm, n_ctx):
    @pl.when(pl.program_id(2) == 0)
    def _():
        h = _mod_norm(x_ref, g_ref, modb_ref, modc_ref, pl.program_id(1) * tm, tm, n_ctx, 0, 1)
        h_ref[...] = h.astype(BF16)

    o_ref[0] = _dot(h_ref[...], w_ref[...])


def _inproj_call(x, modb, modc, g, w, n_ctx):
    b, t, d = x.shape
    tm = _pick(t, (1152, 768, 512, 384, 256, 128))
    tn = _pick(P_WIDTH, (1408, 2816, 704))
    return pl.pallas_call(
        functools.partial(_inproj_kernel, tm=tm, n_ctx=n_ctx),
        grid=(b, t // tm, P_WIDTH // tn),
        in_specs=[pl.BlockSpec((1, tm, d), lambda bi, i, j: (bi, i, 0)),
                  pl.BlockSpec((1, 6, d), lambda bi, i, j: (bi, 0, 0)),
                  pl.BlockSpec((6, d), lambda bi, i, j: (0, 0)),
                  pl.BlockSpec((1, d), lambda bi, i, j: (0, 0)),
                  pl.BlockSpec((d, tn), lambda bi, i, j: (0, j))],
        out_specs=pl.BlockSpec((1, tm, tn), lambda bi, i, j: (bi, i, j)),
        out_shape=jax.ShapeDtypeStruct((b, t, P_WIDTH), F32),
        scratch_shapes=[pltpu.VMEM((tm, d), BF16)],
        compiler_params=_cp(('parallel', 'parallel', 'arbitrary')),
        name='inproj',
    )(x, modb, modc, g, w)


def _chunk_of_step(step, n_chunks, n_ctx_chunks, rev):
    if not rev:
        return step
    return jnp.where(step < n_ctx_chunks, n_ctx_chunks - 1 - step, n_chunks + n_ctx_chunks - 1 - step)


def _order_mask(n, rev):
    ti = lax.broadcasted_iota(jnp.int32, (n, n), 0)
    si = lax.broadcasted_iota(jnp.int32, (n, n), 1)
    return (si >= ti) if rev else (si <= ti)


_BLOCK_W = (2, 4, 8, 16, 32, 64)
_N_LEVELS = 7
A_HPS = 2


def _hgrn_tables():
    ti, si = np.meshgrid(np.arange(CHUNK), np.arange(CHUNK), indexing='ij')
    mats = [np.tile((ti // w == si // w) & (si <= ti), (1, 4)) for w in _BLOCK_W]
    lvl = np.full((2, CHUNK, CHUNK), -1, np.int32)
    for d in range(2):
        lvl[d][ti == si] = 0
        for i, w in enumerate((1, 2, 4, 8, 16, 32)):
            same = ti // (2 * w) == si // (2 * w)
            t_late = (ti // w) % 2 == (1 - d)
            s_early = (si // w) % 2 == d
            lvl[d][same & t_late & s_early] = i + 1
    return np.concatenate(mats, axis=0).astype(np.float32), lvl


def _block_total(p, w):
    n = p.shape[0]
    if w >= 8:
        parts = [jnp.broadcast_to(p[e - 1:e, :], (w, p.shape[1])) for e in range(w, n + 1, w)]
        return parts[0] if len(parts) == 1 else jnp.concatenate(parts, axis=0)
    x = p.reshape(n // 8, 8, p.shape[1])
    sub = lax.broadcasted_iota(jnp.int32, (1, 8, 1), 1)
    out = jnp.broadcast_to(x[:, 7:8, :], x.shape)
    for e in range(8 - w - 1, -1, -w):
        out = jnp.where(sub <= e, jnp.broadcast_to(x[:, e:e + 1, :], x.shape), out)
    return out.reshape(n, p.shape[1])


def _hgrn_kernel(v_ref, ff_ref, fb_ref, q_ref, g_ref, llb_ref, l1m_ref, ng_ref, bs_ref, lvl_ref, o_ref,
                 accf_ref, accb_ref, qs_ref, *, n_chunks, n_ctx_chunks):
    n = CHUNK
    nw = len(_BLOCK_W)
    f_refs, acc_refs = (ff_ref, fb_ref), (accf_ref, accb_ref)
    streams = [(hh, d) for hh in range(A_HPS) for d in range(2)]
    lanes = lambda hh: slice(hh * A_DK, (hh + 1) * A_DK)

    def body(step, carry):
        rows = []
        for d in range(2):
            c = _chunk_of_step(step, n_chunks, n_ctx_chunks, d == 1)
            rows.append(pl.ds(pl.multiple_of(c * n, n), n))
        lf, kk, q, v, sums = {}, {}, {}, {}, {}
        for hh, d in streams:
            fpre = f_refs[d][0, rows[d], lanes(hh)]
            l1m = l1m_ref[d:d + 1, lanes(hh)]
            softplus = _softplus_neg_abs(fpre)
            lf[hh, d] = _logaddexp(llb_ref[d:d + 1, lanes(hh)], l1m + (jnp.minimum(fpre, 0.0) - softplus))
            kk[hh, d] = jnp.exp(l1m + (jnp.minimum(-fpre, 0.0) - softplus))
            q[hh, d] = qs_ref[rows[d], lanes(hh)]
            v[hh, d] = v_ref[0, rows[d], lanes(hh)].astype(BF16)
        for hh in range(A_HPS):
            parts = []
            for d in range(2):
                hi = lf[hh, d].astype(BF16)
                lo = (lf[hh, d] - hi.astype(F32)).astype(BF16)
                zero = jnp.zeros_like(hi)
                parts += [jnp.concatenate([hi, zero] if d == 0 else [zero, hi], axis=1),
                          jnp.concatenate([lo, zero] if d == 0 else [zero, lo], axis=1)]
            sums_fb = _dot(bs_ref[...], jnp.concatenate(parts, axis=0))
            sums[hh, 0], sums[hh, 1] = sums_fb[:, 0:A_DK], sums_fb[:, A_DK:2 * A_DK]

        a_lv, o_inter, st_new = {}, {}, []
        for si, (hh, d) in enumerate(streams):
            key = (hh, d)
            pre = lambda i, key=key: sums[key][i * n:(i + 1) * n]
            rest = lambda i, pre=pre: _block_total(pre(i), _BLOCK_W[i]) - pre(i)
            if d == 0:
                eq, ek = pre, rest
            else:
                eq = lambda i, key=key, rest=rest: rest(i) + lf[key]
                ek = lambda i, key=key, pre=pre: pre(i) - lf[key]
            kb = kk[key].astype(BF16)
            q01 = jnp.concatenate([q[key], q[key] * jnp.exp(lf[key])], axis=0).astype(BF16)
            a01 = _dot_nt(q01, kb)
            mats = [a01[0:n], a01[n:2 * n]]
            for i in range(nw - 1):
                ql = (q[key] * jnp.exp(eq(i))).astype(BF16)
                kl = (kk[key] * jnp.exp(ek(i))).astype(BF16)
                mats.append(_dot_nt(ql, kl))
            a_lv[key] = mats
            st = carry[si]
            o_inter[key] = _dot_nt((q[key] * jnp.exp(eq(nw - 1))).astype(BF16), st.astype(BF16))
            kdec = (kk[key] * jnp.exp(ek(nw - 1))).astype(BF16)
            total = sums[key][(nw - 1) * n + n - 1:(nw - 1) * n + n]
            st_new.append(jnp.exp(total) * st + _dot_tn(v[key], kdec))

        o_intra = {}
        for key in streams:
            lvl = lvl_ref[key[1]]
            a = jnp.zeros((n, n), F32)
            for i in range(_N_LEVELS):
                a = jnp.where(lvl == i, a_lv[key][i], a)
            o_intra[key] = _dot(a.astype(BF16), v[key])
        for hh, d in streams:
            acc_refs[d][rows[d], lanes(hh)] = o_inter[hh, d] + o_intra[hh, d]
        return tuple(st_new)

    qs_ref[...] = _silu(q_ref[0])
    zero = jnp.zeros((A_DV, A_DK), F32)
    lax.fori_loop(0, n_chunks, body, (zero,) * len(streams), unroll=4 if n_chunks % 4 == 0 else 2)
    for hh in range(A_HPS):
        o = accf_ref[:, lanes(hh)] + accb_ref[:, lanes(hh)]
        o_ref[0, :, lanes(hh)] = (_rms(o, ng_ref[...]) * _silu(g_ref[0, :, lanes(hh)])).astype(o_ref.dtype)


def _hgrn_call(p, llb, l1m, ng, n_ctx):
    b, t, _ = p.shape
    bs_np, lvl_np = _hgrn_tables()
    bs, lvl = jnp.asarray(bs_np, BF16), jnp.asarray(lvl_np)
    wid = A_HPS * A_DK
    col = lambda base: pl.BlockSpec((1, t, wid), lambda bi, h: (bi, 0, base // A_HPS + h))
    par = pl.BlockSpec((2, wid), lambda bi, h: (0, h))
    return pl.pallas_call(
        functools.partial(_hgrn_kernel, n_chunks=t // CHUNK, n_ctx_chunks=n_ctx // CHUNK),
        grid=(b, A_HEADS // A_HPS),
        in_specs=[col(BLK['a_i']), col(BLK['a_ff']), col(BLK['a_fb']), col(BLK['a_q']), col(BLK['a_g']),
                  par, par, pl.BlockSpec((1, A_DV), lambda bi, h: (0, 0)),
                  pl.BlockSpec(bs.shape, lambda bi, h: (0, 0)), pl.BlockSpec(lvl.shape, lambda bi, h: (0, 0, 0))],
        out_specs=pl.BlockSpec((1, t, wid), lambda bi, h: (bi, 0, h)),
        out_shape=jax.ShapeDtypeStruct((b, t, A_HEADS * A_DV), BF16),
        scratch_shapes=[pltpu.VMEM((t, wid), F32), pltpu.VMEM((t, wid), F32), pltpu.VMEM((t, wid), F32)],
        compiler_params=_cp(('parallel', 'parallel')),
        name='hgrn2',
    )(p, p, p, p, p, llb, l1m, ng, bs, lvl)


def _scan_max(x, rev):
    n = x.shape[0]
    row = lax.broadcasted_iota(jnp.int32, (n, 1), 0)
    k = 1
    while k < n:
        if rev:
            shifted = jnp.where(row < n - k, pltpu.roll(x, n - k, 0), NEG)
        else:
            shifted = jnp.where(row >= k, pltpu.roll(x, k, 0), NEG)
        x = jnp.maximum(x, shifted)
        k *= 2
    return x


def _lane_bcast(x, lane, width):
    return jnp.broadcast_to(x[:, lane:lane + 1], (x.shape[0], width))


def _mlstm_kernel(kp_ref, qp_ref, v_ref, og_ref, misc_ref, cw_ref, gb_ref, ng_ref, tri_ref, o_ref,
                  qs_ref, ks_ref, accf_ref, accb_ref, c_ref, *, t, n_ctx):
    n = CHUNK
    n_chunks, n_ctx_chunks = t // n, n_ctx // n
    row = lax.broadcasted_iota(jnp.int32, (t, 1), 0)
    no_prev = (row == 0) | (row == n_ctx)
    no_next = (row == n_ctx - 1) | (row == t - 1)

    def conv_silu(x, w):
        xm = jnp.where(no_prev, 0.0, pltpu.roll(x, 1, 0))
        xp = jnp.where(no_next, 0.0, pltpu.roll(x, t - 1, 0))
        return _silu(w[0:1, :] * xm + w[1:2, :] * x + w[2:3, :] * xp)

    qs_ref[...] = conv_silu(qp_ref[0], cw_ref[0])
    ks_ref[...] = conv_silu(kp_ref[0], cw_ref[1]) * (B_DQK ** -0.5)
    c_ref[...] = jnp.zeros(c_ref.shape, F32)

    lane = lax.broadcasted_iota(jnp.int32, (1, LANES), 1)
    head_lanes = (lane < B_DQK, lane >= B_DQK)
    row2 = lax.broadcasted_iota(jnp.int32, (2 * B_DQK, 1), 0)
    acc_refs = (accf_ref, accb_ref)
    streams = [(d, j) for d in range(2) for j in range(2)]
    gate_lane = lambda d, j: GATE_LANE0 + 4 * d + j

    def body(step, m_prev):
        rows, g_all, bal_all = [], [], []
        for d in range(2):
            c = _chunk_of_step(step, n_chunks, n_ctx_chunks, d == 1)
            rows.append(pl.ds(pl.multiple_of(c * n, n), n))
            g = misc_ref[0, rows[d], :] + gb_ref[0]
            ls = _log_sigmoid(g)
            hi = ls.astype(BF16)
            lo = (ls - hi.astype(F32)).astype(BF16)
            bc = _dot(tri_ref[d], hi) + _dot(tri_ref[d], lo)
            g_all.append(g)
            bal_all.append(pltpu.roll(bc, LANES - 2, 1))

        qm, kf, qk, qc = {}, {}, {}, {}
        for d, j in streams:
            kf[d] = ks_ref[rows[d], :]
            qm[d, j] = jnp.where(head_lanes[j], qs_ref[rows[d], :], 0.0).astype(BF16)
            qk[d, j] = _dot_nt(qm[d, j], kf[d].astype(BF16))
            qc[d, j] = _dot(qm[d, j], c_ref[d].astype(BF16))

        u_all, ut_all, cu_all, r_all, w_all, em_all, wk_all, dec_all, m_new = [], [], [], [], [], [], [], [], []
        for d in range(2):
            last = 0 if d == 1 else n - 1
            u = g_all[d] - bal_all[d]
            cu = _scan_max(u, d == 1)
            gm = jnp.maximum(m_prev[d], cu)
            w = jnp.exp(m_prev[d] - gm)
            gl = gm[last:last + 1]
            u_all.append(u)
            ut_all.append(jnp.concatenate([u, cu], axis=0).T[:, 0:n])
            cu_all.append(cu)
            r_all.append(jnp.exp(cu - gm))
            w_all.append(w)
            em_all.append(jnp.exp(-bal_all[d] - gm))
            wk_all.append(jnp.exp(u - gl))
            dec_all.append(w[last:last + 1])
            m_new.append(bal_all[d][last:last + 1] + gl)

        un, uk = {}, {}
        for d, j in streams:
            li = gate_lane(d, j)
            expo = ut_all[d][li:li + 1, :] - _lane_bcast(cu_all[d], li, n)
            s0 = (qk[d, j] * jnp.exp(jnp.where(_order_mask(n, d == 1), expo, NEG))).astype(BF16)
            one_col = jnp.broadcast_to(jnp.where(lane == li, 1.0, 0.0), (n, LANES))
            vaug = jnp.concatenate([v_ref[0, rows[d], j * B_DV:(j + 1) * B_DV], one_col], axis=1).astype(BF16)
            un[d, j] = _dot(s0, vaug)
            kw = jnp.where(head_lanes[j], kf[d], 0.0) * _lane_bcast(wk_all[d], li, LANES)
            uk[d, j] = _dot_tn(kw.astype(BF16), vaug)

        for d, j in streams:
            li = gate_lane(d, j)
            num = (_lane_bcast(w_all[d], li, B_DV) * qc[d, j][:, 0:B_DV]
                   + _lane_bcast(r_all[d], li, B_DV) * un[d, j][:, 0:B_DV])
            den = w_all[d] * qc[d, j][:, B_DV:] + r_all[d] * un[d, j][:, B_DV:]
            z = 1.0 / jnp.maximum(jnp.abs(den), em_all[d])
            acc_refs[d][rows[d], j * B_DV:(j + 1) * B_DV] = num * _lane_bcast(z, li, B_DV)
        for d in range(2):
            l0, l1 = gate_lane(d, 0), gate_lane(d, 1)
            dec = jnp.where(row2 < B_DQK, dec_all[d][:, l0:l0 + 1], dec_all[d][:, l1:l1 + 1])
            c_ref[d] = dec * c_ref[d] + uk[d, 0] + uk[d, 1]
        return tuple(m_new)

    m0 = jnp.full((1, LANES), NEG, F32)
    lax.fori_loop(0, n_chunks, body, (m0, m0), unroll=2)

    for j in range(2):
        cols = slice(j * B_DV, (j + 1) * B_DV)
        h = accf_ref[:, cols] + accb_ref[:, cols]
        o_ref[0, :, cols] = (_rms(h, ng_ref[...]) * _sigmoid(og_ref[0, :, cols])).astype(o_ref.dtype)


def _mlstm_call(p, conv_w, gate_bias, ng, n_ctx):
    b, t, _ = p.shape
    pairs = B_HEADS // 2
    tri = jnp.stack([_order_mask(CHUNK, False), _order_mask(CHUNK, True)]).astype(BF16)
    return pl.pallas_call(
        functools.partial(_mlstm_kernel, t=t, n_ctx=n_ctx),
        grid=(b, pairs),
        in_specs=[pl.BlockSpec((1, t, LANES), lambda bi, hp: (bi, 0, BLK['b_k'] + hp)),
                  pl.BlockSpec((1, t, LANES), lambda bi, hp: (bi, 0, BLK['b_q'] + hp)),
                  pl.BlockSpec((1, t, 2 * B_DV), lambda bi, hp: (bi, 0, BLK['b_v'] // 2 + hp)),
                  pl.BlockSpec((1, t, 2 * B_DV), lambda bi, hp: (bi, 0, BLK['b_o'] // 2 + hp)),
                  pl.BlockSpec((1, t, LANES), lambda bi, hp: (bi, 0, BLK['misc0'] + hp)),
                  pl.BlockSpec((2, B_CONV, LANES), lambda bi, hp: (0, 0, hp)),
                  pl.BlockSpec((1, 1, LANES), lambda bi, hp: (hp, 0, 0)),
                  pl.BlockSpec((1, LANES), lambda bi, hp: (0, 0)),
                  pl.BlockSpec((2, CHUNK, CHUNK), lambda bi, hp: (0, 0, 0))],
        out_specs=pl.BlockSpec((1, t, 2 * B_DV), lambda bi, hp: (bi, 0, hp)),
        out_shape=jax.ShapeDtypeStruct((b, t, B_HEADS * B_DV), BF16),
        scratch_shapes=[pltpu.VMEM((t, LANES), F32), pltpu.VMEM((t, LANES), F32),
                        pltpu.VMEM((t, 2 * B_DV), F32), pltpu.VMEM((t, 2 * B_DV), F32),
                        pltpu.VMEM((2, 2 * B_DQK, 2 * B_DV), F32)],
        compiler_params=_cp(('parallel', 'parallel')),
        name='mlstm',
    )(p, p, p, p, p, conv_w, gate_bias, ng, tri)


def _rope_tile(x, cos, sin_signed):
    lane = lax.broadcasted_iota(jnp.int32, (1, LANES), 1)
    first = (lane // (C_DH // 4)) % 2 == 0
    partner = jnp.where(first, pltpu.roll(x, LANES - C_DH // 4, 1), pltpu.roll(x, C_DH // 4, 1))
    return x * cos + partner * sin_signed


def _head_norm_tile(x, g, bd):
    sq = x * x
    hi = sq.astype(BF16)
    lo = (sq - hi.astype(F32)).astype(BF16)
    ms = _dot(hi, bd) + _dot(lo, bd)
    return x * lax.rsqrt(ms + EPS) * g


def _half_to(x, src_hi, dst_hi):
    lane = lax.broadcasted_iota(jnp.int32, (1, LANES), 1)
    if src_hi != dst_hi:
        x = pltpu.roll(x, LANES // 2, 1)
    keep = (lane >= LANES // 2) if dst_hi else (lane < LANES // 2)
    return jnp.where(keep, x, 0.0)


def _gqa_prep_kernel(cq_ref, ck_ref, cv_ref, cos_ref, sin_ref, qg_ref, kg_ref, bd_ref,
                     q_ref, k_ref, v_ref):
    cos, sin, bd = cos_ref[...], sin_ref[...], bd_ref[...]
    k = _head_norm_tile(ck_ref[0], kg_ref[...], bd)
    k_ref[0] = _rope_tile(k, cos, sin).astype(BF16)
    v_ref[0] = cv_ref[0].astype(BF16)
    grp = C_HEADS // C_KV_HEADS
    for mt in range(C_HEADS // 2):
        x = _head_norm_tile(cq_ref[0, :, mt * LANES:(mt + 1) * LANES], qg_ref[...], bd)
        x = _rope_tile(x, cos, sin) * (C_DH ** -0.5)
        for half in range(2):
            head = 2 * mt + half
            q_ref[0, head] = _half_to(x, half == 1, head // grp == 1).astype(BF16)


def _gqa_prep_call(p, cos, sin, qg, kg, bd):
    b, t, _ = p.shape
    tm = _pick(t, (256, 128))
    tok = lambda blk, w: pl.BlockSpec((1, tm, w), lambda bi, i: (bi, i, blk))
    tab = pl.BlockSpec((tm, LANES), lambda bi, i: (i, 0))
    vec = pl.BlockSpec((1, LANES), lambda bi, i: (0, 0))
    kv_spec = pl.BlockSpec((1, tm, LANES), lambda bi, i: (bi, i, 0))
    return pl.pallas_call(
        _gqa_prep_kernel,
        grid=(b, t // tm),
        in_specs=[tok(BLK['c_q'] // 4, 4 * LANES), tok(BLK['c_k'], LANES), tok(BLK['c_v'], LANES),
                  tab, tab, vec, vec, pl.BlockSpec((LANES, LANES), lambda bi, i: (0, 0))],
        out_specs=[pl.BlockSpec((1, C_HEADS, tm, LANES), lambda bi, i: (bi, 0, i, 0)), kv_spec, kv_spec],
        out_shape=[jax.ShapeDtypeStruct((b, C_HEADS, t, LANES), BF16),
                   jax.ShapeDtypeStruct((b, t, LANES), BF16),
                   jax.ShapeDtypeStruct((b, t, LANES), BF16)],
        compiler_params=_cp(('parallel', 'parallel')),
        name='gqa_prep',
    )(p, p, p, cos, sin, qg, kg, bd)


def _softmax_pv(q, k, v, scale):
    s = _dot_nt(q, k)
    if scale is not None:
        s = s * scale
    m = jnp.max(s, axis=-1, keepdims=True)
    e = jnp.exp(s - m)
    l = jnp.sum(e, axis=-1, keepdims=True)
    return _dot(e.astype(BF16), v) / l


def _gqa_attn_kernel(q_ref, k_ref, v_ref, o_ref, *, tq, t, n_ctx):
    grp = C_HEADS // C_KV_HEADS
    lane = lax.broadcasted_iota(jnp.int32, (1, LANES), 1)

    def attend(nk):
        k, v = k_ref[0, 0:nk, :], v_ref[0, 0:nk, :]
        for g in range(C_KV_HEADS):
            q = q_ref[0, g * grp:(g + 1) * grp].reshape(grp * tq, LANES)
            o = _softmax_pv(q, k, v, None)
            for pair in range(grp // 2):
                o_even = o[(2 * pair) * tq:(2 * pair + 1) * tq]
                o_odd = o[(2 * pair + 1) * tq:(2 * pair + 2) * tq]
                lo = o_even if g == 0 else pltpu.roll(o_even, LANES // 2, 1)
                hi = pltpu.roll(o_odd, LANES // 2, 1) if g == 0 else o_odd
                col = (g * grp // 2 + pair) * LANES
                o_ref[0, :, col:col + LANES] = jnp.where(lane < LANES // 2, lo, hi).astype(o_ref.dtype)

    is_ctx = pl.program_id(1) * tq < n_ctx
    pl.when(is_ctx)(lambda: attend(n_ctx))
    pl.when(jnp.logical_not(is_ctx))(lambda: attend(t))


def _gqa_attn_call(q, k, v, n_ctx):
    b, _, t, _ = q.shape
    tq = _pick(n_ctx, (256, 128))
    kv_spec = pl.BlockSpec((1, t, LANES), lambda bi, i: (bi, 0, 0))
    return pl.pallas_call(
        functools.partial(_gqa_attn_kernel, tq=tq, t=t, n_ctx=n_ctx),
        grid=(b, t // tq),
        in_specs=[pl.BlockSpec((1, C_HEADS, tq, LANES), lambda bi, i: (bi, 0, i, 0)), kv_spec, kv_spec],
        out_specs=pl.BlockSpec((1, tq, C_HEADS * C_DH), lambda bi, i: (bi, i, 0)),
        out_shape=jax.ShapeDtypeStruct((b, t, C_HEADS * C_DH), BF16),
        compiler_params=_cp(('parallel', 'parallel')),
        name='gqa_attn',
    )(q, k, v)


def _mla_prep_kernel(ckv_ref, misc_ref, cq_ref, cos_ref, sin_ref, kvg_ref, qg_ref,
                     wuk_ref, wuv_ref, wqn_ref, wqr_ref, q_ref, k_ref, v_ref):
    cos, sin = cos_ref[...], sin_ref[...]
    ckv = _rms(ckv_ref[0], kvg_ref[...]).astype(BF16)
    k_nope = _dot(ckv, wuk_ref[...])
    v_ref[0] = _dot(ckv, wuv_ref[...]).astype(BF16)
    k_rope = _half_to(_rope_tile(misc_ref[0], cos, sin), False, False).astype(BF16)
    cq = _rms(cq_ref[0], qg_ref[...]).astype(BF16)
    q_nope = _dot(cq, wqn_ref[...])
    q_rope = _dot(cq, wqr_ref[...])
    for mt in range(D_HEADS // 2):
        qr = _rope_tile(q_rope[:, mt * LANES:(mt + 1) * LANES], cos, sin)
        for half in range(2):
            h = 2 * mt + half
            q_ref[0, h, :, 0:D_NOPE] = q_nope[:, h * D_NOPE:(h + 1) * D_NOPE].astype(BF16)
            q_ref[0, h, :, D_NOPE:2 * D_NOPE] = _half_to(qr, half == 1, False).astype(BF16)
            k_ref[0, h, :, 0:D_NOPE] = k_nope[:, h * D_NOPE:(h + 1) * D_NOPE].astype(BF16)
            k_ref[0, h, :, D_NOPE:2 * D_NOPE] = k_rope


def _mla_prep_call(p, cos, sin, kvg, qg, wuk, wuv, wqn, wqr):
    b, t, _ = p.shape
    tm = _pick(t, (256, 128))
    tok = lambda blk, w: pl.BlockSpec((1, tm, w), lambda bi, i: (bi, i, blk))
    tab = pl.BlockSpec((tm, LANES), lambda bi, i: (i, 0))
    full = lambda a: pl.BlockSpec(a.shape, lambda bi, i: (0,) * a.ndim)
    hd = 2 * D_NOPE
    qk_spec = pl.BlockSpec((1, D_HEADS, tm, hd), lambda bi, i: (bi, 0, i, 0))
    return pl.pallas_call(
        _mla_prep_kernel,
        grid=(b, t // tm),
        in_specs=[tok(BLK['d_ckv'], LANES), tok(BLK['misc0'], LANES), tok(BLK['d_cq'] // 2, 2 * LANES),
                  tab, tab, full(kvg), full(qg), full(wuk), full(wuv), full(wqn), full(wqr)],
        out_specs=[qk_spec, qk_spec, pl.BlockSpec((1, tm, D_HEADS * D_DV), lambda bi, i: (bi, i, 0))],
        out_shape=[jax.ShapeDtypeStruct((b, D_HEADS, t, hd), BF16),
                   jax.ShapeDtypeStruct((b, D_HEADS, t, hd), BF16),
                   jax.ShapeDtypeStruct((b, t, D_HEADS * D_DV), BF16)],
        compiler_params=_cp(('parallel', 'parallel')),
        name='mla_prep',
    )(p, p, p, cos, sin, kvg, qg, wuk, wuv, wqn, wqr)


def _mla_attn_kernel(q_ref, k_ref, v_ref, o_ref, *, tq, t, n_ctx):
    def attend(nk):
        o_ref[0] = _softmax_pv(q_ref[0, 0], k_ref[0, 0, 0:nk, :], v_ref[0, 0:nk, :],
                               (D_NOPE + D_ROPE) ** -0.5).astype(o_ref.dtype)

    is_ctx = pl.program_id(2) * tq < n_ctx
    pl.when(is_ctx)(lambda: attend(n_ctx))
    pl.when(jnp.logical_not(is_ctx))(lambda: attend(t))


def _mla_attn_call(q, k, v, n_ctx):
    b, _, t, hd = q.shape
    tq = _pick(n_ctx, (256, 128))
    return pl.pallas_call(
        functools.partial(_mla_attn_kernel, tq=tq, t=t, n_ctx=n_ctx),
        grid=(b, D_HEADS, t // tq),
        in_specs=[pl.BlockSpec((1, 1, tq, hd), lambda bi, h, i: (bi, h, i, 0)),
                  pl.BlockSpec((1, 1, t, hd), lambda bi, h, i: (bi, h, 0, 0)),
                  pl.BlockSpec((1, t, D_DV), lambda bi, h, i: (bi, 0, h))],
        out_specs=pl.BlockSpec((1, tq, D_DV), lambda bi, h, i: (bi, i, h)),
        out_shape=jax.ShapeDtypeStruct((b, t, D_HEADS * D_DV), BF16),
        compiler_params=_cp(('parallel', 'parallel', 'parallel')),
        name='mla_attn',
    )(q, k, v)


def _merge_kernel(ya_ref, yb_ref, yc_ref, yd_ref, x_ref, modb_ref, modc_ref, g_ref, wg_ref, wb_ref, wo_ref,
                  o_ref, *, tm, n_ctx):
    row0 = pl.program_id(1) * tm
    h = _mod_norm(x_ref, g_ref, modb_ref, modc_ref, row0, tm, n_ctx, 0, 1).astype(BF16)
    acc = None
    for r, y_ref in enumerate((ya_ref, yb_ref, yc_ref, yd_ref)):
        term = _sigmoid(_dot(h, wg_ref[r])) * _dot(y_ref[0], wb_ref[r])
        acc = term if acc is None else acc + term
    z = _dot(acc.astype(BF16), wo_ref[...])
    o_ref[0] = x_ref[0] + _res_gate(modb_ref, modc_ref, row0, tm, n_ctx, 2) * z


def _merge_call(ys, x, modb, modc, g, wg, wb, wo, n_ctx):
    b, t, d = x.shape
    tm = _pick(t, (256, 128))
    y_spec = pl.BlockSpec((1, tm, BRANCH_W), lambda bi, i: (bi, i, 0))
    x_spec = pl.BlockSpec((1, tm, d), lambda bi, i: (bi, i, 0))
    return pl.pallas_call(
        functools.partial(_merge_kernel, tm=tm, n_ctx=n_ctx),
        grid=(b, t // tm),
        in_specs=[y_spec] * 4 + [
            x_spec, pl.BlockSpec((1, 6, d), lambda bi, i: (bi, 0, 0)), pl.BlockSpec((6, d), lambda bi, i: (0, 0)),
            pl.BlockSpec((1, d), lambda bi, i: (0, 0)),
            pl.BlockSpec((N_BRANCH, d, d), lambda bi, i: (0, 0, 0)),
            pl.BlockSpec((N_BRANCH, BRANCH_W, d), lambda bi, i: (0, 0, 0)),
            pl.BlockSpec((d, d), lambda bi, i: (0, 0))],
        out_specs=x_spec,
        out_shape=jax.ShapeDtypeStruct((b, t, d), F32),
        compiler_params=_cp(('parallel', 'parallel')),
        name='merge',
    )(*ys, x, modb, modc, g, wg, wb, wo)


def _ffn_kernel(x_ref, modb_ref, modc_ref, g_ref, w1_ref, w2_ref, o_ref, h_ref, acc_ref, *, tm, n_ctx):
    j = pl.program_id(2)
    row0 = pl.program_id(1) * tm

    @pl.when(j == 0)
    def _():
        h_ref[...] = _mod_norm(x_ref, g_ref, modb_ref, modc_ref, row0, tm, n_ctx, 3, 4).astype(BF16)

    u = jnp.square(jnp.maximum(_dot(h_ref[...], w1_ref[...]), 0.0))
    part = _dot(u.astype(BF16), w2_ref[...])

    @pl.when(j == 0)
    def _():
        acc_ref[...] = part

    @pl.when(j > 0)
    def _():
        acc_ref[...] = acc_ref[...] + part

    @pl.when(j == pl.num_programs(2) - 1)
    def _():
        o_ref[0] = x_ref[0] + _res_gate(modb_ref, modc_ref, row0, tm, n_ctx, 5) * acc_ref[...]


def _ffn_call(x, modb, modc, g, w1, w2, n_ctx):
    b, t, d = x.shape
    tm = _pick(t, (1152, 768, 512, 384, 256, 128))
    tf = 1024
    x_spec = pl.BlockSpec((1, tm, d), lambda bi, i, j: (bi, i, 0))
    return pl.pallas_call(
        functools.partial(_ffn_kernel, tm=tm, n_ctx=n_ctx),
        grid=(b, t // tm, D_FF // tf),
        in_specs=[x_spec, pl.BlockSpec((1, 6, d), lambda bi, i, j: (bi, 0, 0)),
                  pl.BlockSpec((6, d), lambda bi, i, j: (0, 0)), pl.BlockSpec((1, d), lambda bi, i, j: (0, 0)),
                  pl.BlockSpec((d, tf), lambda bi, i, j: (0, j)), pl.BlockSpec((tf, d), lambda bi, i, j: (j, 0))],
        out_specs=x_spec,
        out_shape=jax.ShapeDtypeStruct((b, t, d), F32),
        scratch_shapes=[pltpu.VMEM((tm, d), BF16), pltpu.VMEM((tm, d), F32)],
        compiler_params=_cp(('parallel', 'parallel', 'arbitrary')),
        name='ffn',
    )(x, modb, modc, g, w1, w2)


def _final_kernel(x_ref, g_ref, o_ref):
    o_ref[0] = _rms(x_ref[0], g_ref[...])


def _final_call(x, g, n_ctx):
    b, t, d = x.shape
    tm = _pick(n_ctx, (256, 128))
    off = n_ctx // tm
    return pl.pallas_call(
        _final_kernel,
        grid=(b, (t - n_ctx) // tm),
        in_specs=[pl.BlockSpec((1, tm, d), lambda bi, i: (bi, i + off, 0)), pl.BlockSpec((1, d), lambda bi, i: (0, 0))],
        out_specs=pl.BlockSpec((1, tm, d), lambda bi, i: (bi, i, 0)),
        out_shape=jax.ShapeDtypeStruct((b, t - n_ctx, d), F32),
        compiler_params=_cp(('parallel', 'parallel')),
        name='final_norm',
    )(x, g)


def _pack_w_in(w_in):
    depth, d, _ = w_in.shape
    ref = lambda n: w_in[:, :, _REF_OFF[n][0]:_REF_OFF[n][0] + _REF_OFF[n][1]]
    zeros = lambda w: jnp.zeros((depth, d, w), w_in.dtype)
    gates = ref('b_gates').reshape(depth, d, 4, B_HEADS)
    pair_gates = [gates[:, :, :, 2 * hp:2 * hp + 2].reshape(depth, d, 8) for hp in range(2)]
    misc0 = jnp.concatenate([ref('d_krope'), pair_gates[0], zeros(LANES - 72)], axis=-1)
    misc1 = jnp.concatenate([zeros(64), pair_gates[1], zeros(LANES - 72)], axis=-1)
    parts = [ref('a_i'), ref('a_f_fwd'), ref('a_f_bwd'), ref('a_q'), ref('a_g'), ref('b_v'),
             ref('b_o'), ref('c_q'), ref('b_k'), ref('b_q'), ref('d_cq'), ref('c_k'), ref('c_v'), ref('d_ckv'),
             misc0, misc1, zeros(LANES)]
    packed = jnp.concatenate(parts, axis=-1)
    assert packed.shape[-1] == P_WIDTH
    return packed.astype(BF16)


def _rope_tables(n_lat, n_ctx):
    rows = n_lat // GRID_W
    row = jnp.repeat(jnp.arange(rows, dtype=jnp.int32), GRID_W)
    col = jnp.broadcast_to(jnp.arange(GRID_W, dtype=jnp.int32), (rows, GRID_W)).reshape(-1)
    quarter = C_DH // 4
    inv_freq = ROPE_THETA ** (-jnp.arange(quarter, dtype=F32) / quarter)
    ang_r = row.astype(F32)[:, None] * inv_freq
    ang_c = col.astype(F32)[:, None] * inv_freq
    cr, sr, cc, sc = jnp.cos(ang_r), jnp.sin(ang_r), jnp.cos(ang_c), jnp.sin(ang_c)
    cos = jnp.concatenate([cr, cr, cc, cc], axis=-1)
    sin = jnp.concatenate([-sr, sr, -sc, sc], axis=-1)
    cos = jnp.concatenate([jnp.ones((n_ctx, C_DH), F32), cos], axis=0)
    sin = jnp.concatenate([jnp.zeros((n_ctx, C_DH), F32), sin], axis=0)
    return jnp.tile(cos, (1, 2)), jnp.tile(sin, (1, 2))


def kernel(x, c, ctx, c_ctx, w_ada, b_ada, g_norm1, g_norm2, w_in, b_mlstm_gates, hgrn_lb_logits,
           hgrn_norm_g, mlstm_conv_w, mlstm_norm_g, gqa_q_norm_g, gqa_k_norm_g, mla_q_norm_g,
           mla_kv_norm_g, w_mla_uq, w_mla_uk, w_mla_uv, w_branch, w_out, w_ff1, w_ff2, g_final):
    bsz, n_lat, d = x.shape
    n_ctx = ctx.shape[1]
    depth = w_ada.shape[0]
    assert d == D_MODEL and n_lat % GRID_W == 0 and n_ctx % 128 == 0 and n_lat % 128 == 0

    w_in_p = _pack_w_in(w_in)
    w_ada_b = w_ada.astype(BF16)
    wb_b, wo_b = w_branch.astype(BF16), w_out.astype(BF16)
    g0, gw = _REF_OFF['gates']
    wg_b = w_in[:, :, g0:g0 + gw].reshape(depth, d, N_BRANCH, d).transpose(0, 2, 1, 3).astype(BF16)
    w1_b, w2_b = w_ff1.astype(BF16), w_ff2.astype(BF16)
    wuk_b, wuv_b = w_mla_uk.astype(BF16), w_mla_uv.astype(BF16)
    uq = w_mla_uq.reshape(depth, D_Q_LORA, D_HEADS, D_NOPE + D_ROPE)
    wqn_b = uq[..., :D_NOPE].reshape(depth, D_Q_LORA, D_HEADS * D_NOPE).astype(BF16)
    wqr_b = uq[..., D_NOPE:].reshape(depth, D_Q_LORA, D_HEADS * D_ROPE).astype(BF16)
    cos_t, sin_t = _rope_tables(n_lat, n_ctx)
    half = lax.broadcasted_iota(jnp.int32, (LANES, LANES), 0) // C_DH == lax.broadcasted_iota(
        jnp.int32, (LANES, LANES), 1) // C_DH
    bd = jnp.where(half, 1.0 / C_DH, 0.0).astype(BF16)
    lb = jnp.cumsum(jax.nn.softmax(hgrn_lb_logits.astype(F32), axis=0), axis=0)
    lb = lb - lb[0]
    llb, l1m = jnp.log(lb), jnp.log1p(-lb)
    gb = b_mlstm_gates.reshape(depth, 4, B_HEADS)
    gate_bias = jnp.zeros((depth, 2, 1, LANES), F32)
    for hp in range(2):
        gate_bias = gate_bias.at[:, hp, 0, GATE_LANE0:GATE_LANE0 + 8].set(gb[:, :, 2 * hp:2 * hp + 2].reshape(depth, 8))
    tile2 = lambda g: jnp.tile(g, (1, 2))

    pad = (-(bsz + 1)) % 8
    s_rows = jnp.concatenate([c, c_ctx[None, :], jnp.zeros((pad, d), F32)], axis=0)
    mods = _ada_call(s_rows, w_ada_b, b_ada)

    xa = jnp.concatenate([ctx, x], axis=1)
    for l in range(depth):
        modb = mods[l, :bsz].reshape(bsz, 6, d)
        modc = mods[l, bsz].reshape(6, d)
        p = _inproj_call(xa, modb, modc, g_norm1[l][None, :], w_in_p[l], n_ctx)
        ya = _hgrn_call(p, llb[l], l1m[l], hgrn_norm_g[l][None, :], n_ctx)
        yb = _mlstm_call(p, mlstm_conv_w[l], gate_bias[l], mlstm_norm_g[l][None, :], n_ctx)
        qc, kc, vc = _gqa_prep_call(p, cos_t, sin_t, tile2(gqa_q_norm_g[l][None, :]),
                                    tile2(gqa_k_norm_g[l][None, :]), bd)
        yc = _gqa_attn_call(qc, kc, vc, n_ctx)
        qd, kd, vd = _mla_prep_call(p, cos_t, sin_t, mla_kv_norm_g[l][None, :], mla_q_norm_g[l][None, :],
                                    wuk_b[l], wuv_b[l], wqn_b[l], wqr_b[l])
        yd = _mla_attn_call(qd, kd, vd, n_ctx)
        xa = _merge_call((ya, yb, yc, yd), xa, modb, modc, g_norm1[l][None, :], wg_b[l], wb_b[l], wo_b[l], n_ctx)
        xa = _ffn_call(xa, modb, modc, g_norm2[l][None, :], w1_b[l], w2_b[l], n_ctx)
    return _final_call(xa, g_final[None, :], n_ctx)
```

```python
import functools

import numpy as np
import jax
import jax.numpy as jnp
from jax import lax
from jax.experimental import pallas as pl
from jax.experimental.pallas import tpu as pltpu

F32 = jnp.float32
BF16 = jnp.bfloat16

D_MODEL = 1024
GRID_W = 64
EPS = 1e-6
NEG = -1e30
ROPE_THETA = 10000.0
CHUNK = 64
LANES = 128

A_HEADS, A_DK, A_DV = 4, 128, 128
B_HEADS, B_DQK, B_DV, B_CONV = 4, 64, 128, 3
C_HEADS, C_KV_HEADS, C_DH = 8, 2, 64
D_HEADS, D_Q_LORA, D_KV_LORA, D_NOPE, D_ROPE, D_DV = 4, 256, 128, 128, 64, 128
N_BRANCH, BRANCH_W = 4, 512
D_FF = 4 * D_MODEL

_REF_COLS = (
    ('a_i', 512), ('a_f_fwd', 512), ('a_f_bwd', 512), ('b_k', 256), ('b_v', 512), ('b_gates', 16),
    ('c_k', 128), ('c_v', 128), ('d_ckv', 128), ('d_krope', 64),
    ('a_q', 512), ('a_g', 512), ('b_q', 256), ('b_o', 512), ('c_q', 512), ('d_cq', 256), ('gates', 4096),
)
_REF_OFF = {}
_o = 0
for _n, _w in _REF_COLS:
    _REF_OFF[_n] = (_o, _w)
    _o += _w

BLK = dict(a_i=0, a_ff=4, a_fb=8, a_q=12, a_g=16, b_v=20, b_o=24, c_q=28, b_k=32, b_q=34,
           d_cq=36, c_k=38, c_v=39, d_ckv=40, misc0=41, misc1=42)
P_BLOCKS = 44
P_WIDTH = P_BLOCKS * LANES
GATE_LANE0 = 64

VMEM_LIMIT = 56 * 1024 * 1024


def _cp(sem, **kw):
    return pltpu.CompilerParams(dimension_semantics=sem, vmem_limit_bytes=VMEM_LIMIT, **kw)


def _pick(n, cands):
    for c in cands:
        if n % c == 0:
            return c
    raise ValueError(f'no tile for {n}')


def _sigmoid(x):
    return 1.0 / (1.0 + jnp.exp(-x))


def _silu(x):
    return x * _sigmoid(x)


def _softplus_neg_abs(x):
    return jnp.log(1.0 + jnp.exp(-jnp.abs(x)))


def _rms(xf, g):
    ms = jnp.mean(xf * xf, axis=-1, keepdims=True)
    return xf * lax.rsqrt(ms + EPS) * g


def _log_sigmoid(x):
    return jnp.minimum(x, 0.0) - _softplus_neg_abs(x)


def _logaddexp(a, b):
    return jnp.maximum(a, b) + _softplus_neg_abs(a - b)


def _dot(a, b):
    return jnp.dot(a, b, preferred_element_type=F32)


def _dot_nt(a, b):
    return lax.dot_general(a, b, (((1,), (1,)), ((), ())), preferred_element_type=F32)


def _dot_tn(a, b):
    return lax.dot_general(a, b, (((0,), (0,)), ((), ())), preferred_element_type=F32)


def _mod_norm(x_ref, g_ref, modb_ref, modc_ref, row0, tm, n_ctx, k_shift, k_scale):
    y = _rms(x_ref[0], g_ref[...])
    row = row0 + lax.broadcasted_iota(jnp.int32, (tm, 1), 0)
    is_ctx = row < n_ctx
    shift = jnp.where(is_ctx, modc_ref[k_shift:k_shift + 1, :], modb_ref[0, k_shift:k_shift + 1, :])
    scale = jnp.where(is_ctx, modc_ref[k_scale:k_scale + 1, :], modb_ref[0, k_scale:k_scale + 1, :])
    return y * (1.0 + scale) + shift


def _res_gate(modb_ref, modc_ref, row0, tm, n_ctx, k):
    row = row0 + lax.broadcasted_iota(jnp.int32, (tm, 1), 0)
    return jnp.where(row < n_ctx, modc_ref[k:k + 1, :], modb_ref[0, k:k + 1, :])


def _ada_kernel(s_ref, w_ref, b_ref, o_ref):
    s = _silu(s_ref[...])
    o_ref[0] = _dot(s.astype(BF16), w_ref[0]) + b_ref[0]


def _ada_call(s_rows, w_ada, b_ada):
    depth, d, d6 = w_ada.shape
    r = s_rows.shape[0]
    return pl.pallas_call(
        _ada_kernel,
        grid=(depth, d6 // d),
        in_specs=[pl.BlockSpec((r, d), lambda l, j: (0, 0)),
                  pl.BlockSpec((1, d, d), lambda l, j: (l, 0, j)),
                  pl.BlockSpec((1, 1, d), lambda l, j: (l, 0, j))],
        out_specs=pl.BlockSpec((1, r, d), lambda l, j: (l, 0, j)),
        out_shape=jax.ShapeDtypeStruct((depth, r, d6), F32),
        compiler_params=_cp(('parallel', 'parallel')),
        name='ada',
    )(s_rows, w_ada, b_ada.reshape(depth, 1, d6))


def _inproj_kernel(x_ref, modb_ref, modc_ref, g_ref, w_ref, o_ref, h_ref, *, tm, n_ctx):
    @pl.when(pl.program_id(2) == 0)
    def _():
        h = _mod_norm(x_ref, g_ref, modb_ref, modc_ref, pl.program_id(1) * tm, tm, n_ctx, 0, 1)
        h_ref[...] = h.astype(BF16)

    o_ref[0] = _dot(h_ref[...], w_ref[...])


def _inproj_call(x, modb, modc, g, w, n_ctx):
    b, t, d = x.shape
    tm = _pick(t, (1152, 768, 512, 384, 256, 128))
    tn = _pick(P_WIDTH, (1408, 2816, 704))
    return pl.pallas_call(
        functools.partial(_inproj_kernel, tm=tm, n_ctx=n_ctx),
        grid=(b, t // tm, P_WIDTH // tn),
        in_specs=[pl.BlockSpec((1, tm, d), lambda bi, i, j: (bi, i, 0)),
                  pl.BlockSpec((1, 6, d), lambda bi, i, j: (bi, 0, 0)),
                  pl.BlockSpec((6, d), lambda bi, i, j: (0, 0)),
                  pl.BlockSpec((1, d), lambda bi, i, j: (0, 0)),
                  pl.BlockSpec((d, tn), lambda bi, i, j: (0, j))],
        out_specs=pl.BlockSpec((1, tm, tn), lambda bi, i, j: (bi, i, j)),
        out_shape=jax.ShapeDtypeStruct((b, t, P_WIDTH), F32),
        scratch_shapes=[pltpu.VMEM((tm, d), BF16)],
        compiler_params=_cp(('parallel', 'parallel', 'arbitrary')),
        name='inproj',
    )(x, modb, modc, g, w)


def _chunk_of_step(step, n_chunks, n_ctx_chunks, rev):
    if not rev:
        return step
    return jnp.where(step < n_ctx_chunks, n_ctx_chunks - 1 - step, n_chunks + n_ctx_chunks - 1 - step)


def _order_mask(n, rev):
    ti = lax.broadcasted_iota(jnp.int32, (n, n), 0)
    si = lax.broadcasted_iota(jnp.int32, (n, n), 1)
    return (si >= ti) if rev else (si <= ti)


_BLOCK_W = (2, 4, 8, 16, 32, 64)
_N_LEVELS = 7
A_HPS = 2


def _hgrn_tables():
    ti, si = np.meshgrid(np.arange(CHUNK), np.arange(CHUNK), indexing='ij')
    mats = [np.tile((ti // w == si // w) & (si <= ti), (1, 4)) for w in _BLOCK_W]
    lvl = np.full((2, CHUNK, CHUNK), -1, np.int32)
    for d in range(2):
        lvl[d][ti == si] = 0
        for i, w in enumerate((1, 2, 4, 8, 16, 32)):
            same = ti // (2 * w) == si // (2 * w)
            t_late = (ti // w) % 2 == (1 - d)
            s_early = (si // w) % 2 == d
            lvl[d][same & t_late & s_early] = i + 1
    return np.concatenate(mats, axis=0).astype(np.float32), lvl


def _block_total(p, w):
    n = p.shape[0]
    if w >= 8:
        parts = [jnp.broadcast_to(p[e - 1:e, :], (w, p.shape[1])) for e in range(w, n + 1, w)]
        return parts[0] if len(parts) == 1 else jnp.concatenate(parts, axis=0)
    x = p.reshape(n // 8, 8, p.shape[1])
    sub = lax.broadcasted_iota(jnp.int32, (1, 8, 1), 1)
    out = jnp.broadcast_to(x[:, 7:8, :], x.shape)
    for e in range(8 - w - 1, -1, -w):
        out = jnp.where(sub <= e, jnp.broadcast_to(x[:, e:e + 1, :], x.shape), out)
    return out.reshape(n, p.shape[1])


def _hgrn_kernel(v_ref, ff_ref, fb_ref, q_ref, g_ref, llb_ref, l1m_ref, ng_ref, bs_ref, lvl_ref, o_ref,
                 accf_ref, accb_ref, qs_ref, *, n_chunks, n_ctx_chunks):
    n = CHUNK
    nw = len(_BLOCK_W)
    f_refs, acc_refs = (ff_ref, fb_ref), (accf_ref, accb_ref)
    streams = [(hh, d) for hh in range(A_HPS) for d in range(2)]
    lanes = lambda hh: slice(hh * A_DK, (hh + 1) * A_DK)

    def body(step, carry):
        rows = []
        for d in range(2):
            c = _chunk_of_step(step, n_chunks, n_ctx_chunks, d == 1)
            rows.append(pl.ds(pl.multiple_of(c * n, n), n))
        lf, kk, q, v, sums = {}, {}, {}, {}, {}
        for hh, d in streams:
            fpre = f_refs[d][0, rows[d], lanes(hh)]
            l1m = l1m_ref[d:d + 1, lanes(hh)]
            softplus = _softplus_neg_abs(fpre)
            lf[hh, d] = _logaddexp(llb_ref[d:d + 1, lanes(hh)], l1m + (jnp.minimum(fpre, 0.0) - softplus))
            kk[hh, d] = jnp.exp(l1m + (jnp.minimum(-fpre, 0.0) - softplus))
            q[hh, d] = qs_ref[rows[d], lanes(hh)]
            v[hh, d] = v_ref[0, rows[d], lanes(hh)].astype(BF16)
        for hh in range(A_HPS):
            parts = []
            for d in range(2):
                hi = lf[hh, d].astype(BF16)
                lo = (lf[hh, d] - hi.astype(F32)).astype(BF16)
                zero = jnp.zeros_like(hi)
                parts += [jnp.concatenate([hi, zero] if d == 0 else [zero, hi], axis=1),
                          jnp.concatenate([lo, zero] if d == 0 else [zero, lo], axis=1)]
            sums_fb = _dot(bs_ref[...], jnp.concatenate(parts, axis=0))
            sums[hh, 0], sums[hh, 1] = sums_fb[:, 0:A_DK], sums_fb[:, A_DK:2 * A_DK]

        a_lv, o_inter, st_new = {}, {}, []
        for si, (hh, d) in enumerate(streams):
            key = (hh, d)
            pre = lambda i, key=key: sums[key][i * n:(i + 1) * n]
            rest = lambda i, pre=pre: _block_total(pre(i), _BLOCK_W[i]) - pre(i)
            if d == 0:
                eq, ek = pre, rest
            else:
                eq = lambda i, key=key, rest=rest: rest(i) + lf[key]
                ek = lambda i, key=key, pre=pre: pre(i) - lf[key]
            kb = kk[key].astype(BF16)
            q01 = jnp.concatenate([q[key], q[key] * jnp.exp(lf[key])], axis=0).astype(BF16)
            a01 = _dot_nt(q01, kb)
            mats = [a01[0:n], a01[n:2 * n]]
            for i in range(nw - 1):
                ql = (q[key] * jnp.exp(eq(i))).astype(BF16)
                kl = (kk[key] * jnp.exp(ek(i))).astype(BF16)
                mats.append(_dot_nt(ql, kl))
            a_lv[key] = mats
            st = carry[si]
            o_inter[key] = _dot_nt((q[key] * jnp.exp(eq(nw - 1))).astype(BF16), st.astype(BF16))
            kdec = (kk[key] * jnp.exp(ek(nw - 1))).astype(BF16)
            total = sums[key][(nw - 1) * n + n - 1:(nw - 1) * n + n]
            st_new.append(jnp.exp(total) * st + _dot_tn(v[key], kdec))

        o_intra = {}
        for key in streams:
            lvl = lvl_ref[key[1]]
            a = jnp.zeros((n, n), F32)
            for i in range(_N_LEVELS):
                a = jnp.where(lvl == i, a_lv[key][i], a)
            o_intra[key] = _dot(a.astype(BF16), v[key])
        for hh, d in streams:
            acc_refs[d][rows[d], lanes(hh)] = o_inter[hh, d] + o_intra[hh, d]
        return tuple(st_new)

    qs_ref[...] = _silu(q_ref[0])
    zero = jnp.zeros((A_DV, A_DK), F32)
    lax.fori_loop(0, n_chunks, body, (zero,) * len(streams), unroll=4 if n_chunks % 4 == 0 else 2)
    for hh in range(A_HPS):
        o = accf_ref[:, lanes(hh)] + accb_ref[:, lanes(hh)]
        o_ref[0, :, lanes(hh)] = (_rms(o, ng_ref[...]) * _silu(g_ref[0, :, lanes(hh)])).astype(o_ref.dtype)


def _hgrn_call(p, llb, l1m, ng, n_ctx):
    b, t, _ = p.shape
    bs_np, lvl_np = _hgrn_tables()
    bs, lvl = jnp.asarray(bs_np, BF16), jnp.asarray(lvl_np)
    wid = A_HPS * A_DK
    col = lambda base: pl.BlockSpec((1, t, wid), lambda bi, h: (bi, 0, base // A_HPS + h))
    par = pl.BlockSpec((2, wid), lambda bi, h: (0, h))
    return pl.pallas_call(
        functools.partial(_hgrn_kernel, n_chunks=t // CHUNK, n_ctx_chunks=n_ctx // CHUNK),
        grid=(b, A_HEADS // A_HPS),
        in_specs=[col(BLK['a_i']), col(BLK['a_ff']), col(BLK['a_fb']), col(BLK['a_q']), col(BLK['a_g']),
                  par, par, pl.BlockSpec((1, A_DV), lambda bi, h: (0, 0)),
                  pl.BlockSpec(bs.shape, lambda bi, h: (0, 0)), pl.BlockSpec(lvl.shape, lambda bi, h: (0, 0, 0))],
        out_specs=pl.BlockSpec((1, t, wid), lambda bi, h: (bi, 0, h)),
        out_shape=jax.ShapeDtypeStruct((b, t, A_HEADS * A_DV), BF16),
        scratch_shapes=[pltpu.VMEM((t, wid), F32), pltpu.VMEM((t, wid), F32), pltpu.VMEM((t, wid), F32)],
        compiler_params=_cp(('parallel', 'parallel')),
        name='hgrn2',
    )(p, p, p, p, p, llb, l1m, ng, bs, lvl)


def _scan_max(x, rev):
    n = x.shape[0]
    row = lax.broadcasted_iota(jnp.int32, (n, 1), 0)
    k = 1
    while k < n:
        if rev:
            shifted = jnp.where(row < n - k, pltpu.roll(x, n - k, 0), NEG)
        else:
            shifted = jnp.where(row >= k, pltpu.roll(x, k, 0), NEG)
        x = jnp.maximum(x, shifted)
        k *= 2
    return x


def _lane_bcast(x, lane, width):
    return jnp.broadcast_to(x[:, lane:lane + 1], (x.shape[0], width))


def _mlstm_kernel(kp_ref, qp_ref, v_ref, og_ref, misc_ref, cw_ref, gb_ref, ng_ref, tri_ref, o_ref,
                  qs_ref, ks_ref, accf_ref, accb_ref, c_ref, *, t, n_ctx):
    n = CHUNK
    n_chunks, n_ctx_chunks = t // n, n_ctx // n
    row = lax.broadcasted_iota(jnp.int32, (t, 1), 0)
    no_prev = (row == 0) | (row == n_ctx)
    no_next = (row == n_ctx - 1) | (row == t - 1)

    def conv_silu(x, w):
        xm = jnp.where(no_prev, 0.0, pltpu.roll(x, 1, 0))
        xp = jnp.where(no_next, 0.0, pltpu.roll(x, t - 1, 0))
        return _silu(w[0:1, :] * xm + w[1:2, :] * x + w[2:3, :] * xp)

    qs_ref[...] = conv_silu(qp_ref[0], cw_ref[0])
    ks_ref[...] = conv_silu(kp_ref[0], cw_ref[1]) * (B_DQK ** -0.5)
    c_ref[...] = jnp.zeros(c_ref.shape, F32)

    lane = lax.broadcasted_iota(jnp.int32, (1, LANES), 1)
    head_lanes = (lane < B_DQK, lane >= B_DQK)
    row2 = lax.broadcasted_iota(jnp.int32, (2 * B_DQK, 1), 0)
    acc_refs = (accf_ref, accb_ref)
    streams = [(d, j) for d in range(2) for j in range(2)]
    gate_lane = lambda d, j: GATE_LANE0 + 4 * d + j

    def body(step, m_prev):
        rows, g_all, bal_all = [], [], []
        for d in range(2):
            c = _chunk_of_step(step, n_chunks, n_ctx_chunks, d == 1)
            rows.append(pl.ds(pl.multiple_of(c * n, n), n))
            g = misc_ref[0, rows[d], :] + gb_ref[0]
            ls = _log_sigmoid(g)
            hi = ls.astype(BF16)
            lo = (ls - hi.astype(F32)).astype(BF16)
            bc = _dot(tri_ref[d], hi) + _dot(tri_ref[d], lo)
            g_all.append(g)
            bal_all.append(pltpu.roll(bc, LANES - 2, 1))

        qm, kf, qk, qc = {}, {}, {}, {}
        for d, j in streams:
            kf[d] = ks_ref[rows[d], :]
            qm[d, j] = jnp.where(head_lanes[j], qs_ref[rows[d], :], 0.0).astype(BF16)
            qk[d, j] = _dot_nt(qm[d, j], kf[d].astype(BF16))
            qc[d, j] = _dot(qm[d, j], c_ref[d].astype(BF16))

        u_all, ut_all, cu_all, r_all, w_all, em_all, wk_all, dec_all, m_new = [], [], [], [], [], [], [], [], []
        for d in range(2):
            last = 0 if d == 1 else n - 1
            u = g_all[d] - bal_all[d]
            cu = _scan_max(u, d == 1)
            gm = jnp.maximum(m_prev[d], cu)
            w = jnp.exp(m_prev[d] - gm)
            gl = gm[last:last + 1]
            u_all.append(u)
            ut_all.append(jnp.concatenate([u, cu], axis=0).T[:, 0:n])
            cu_all.append(cu)
            r_all.append(jnp.exp(cu - gm))
            w_all.append(w)
            em_all.append(jnp.exp(-bal_all[d] - gm))
            wk_all.append(jnp.exp(u - gl))
            dec_all.append(w[last:last + 1])
            m_new.append(bal_all[d][last:last + 1] + gl)

        un, uk = {}, {}
        for d, j in streams:
            li = gate_lane(d, j)
            expo = ut_all[d][li:li + 1, :] - _lane_bcast(cu_all[d], li, n)
            s0 = (qk[d, j] * jnp.exp(jnp.where(_order_mask(n, d == 1), expo, NEG))).astype(BF16)
            one_col = jnp.broadcast_to(jnp.where(lane == li, 1.0, 0.0), (n, LANES))
            vaug = jnp.concatenate([v_ref[0, rows[d], j * B_DV:(j + 1) * B_DV], one_col], axis=1).astype(BF16)
            un[d, j] = _dot(s0, vaug)
            kw = jnp.where(head_lanes[j], kf[d], 0.0) * _lane_bcast(wk_all[d], li, LANES)
            uk[d, j] = _dot_tn(kw.astype(BF16), vaug)

        for d, j in streams:
            li = gate_lane(d, j)
            num = (_lane_bcast(w_all[d], li, B_DV) * qc[d, j][:, 0:B_DV]
                   + _lane_bcast(r_all[d], li, B_DV) * un[d, j][:, 0:B_DV])
            den = w_all[d] * qc[d, j][:, B_DV:] + r_all[d] * un[d, j][:, B_DV:]
            z = 1.0 / jnp.maximum(jnp.abs(den), em_all[d])
            acc_refs[d][rows[d], j * B_DV:(j + 1) * B_DV] = num * _lane_bcast(z, li, B_DV)
        for d in range(2):
            l0, l1 = gate_lane(d, 0), gate_lane(d, 1)
            dec = jnp.where(row2 < B_DQK, dec_all[d][:, l0:l0 + 1], dec_all[d][:, l1:l1 + 1])
            c_ref[d] = dec * c_ref[d] + uk[d, 0] + uk[d, 1]
        return tuple(m_new)

    m0 = jnp.full((1, LANES), NEG, F32)
    lax.fori_loop(0, n_chunks, body, (m0, m0), unroll=2)

    for j in range(2):
        cols = slice(j * B_DV, (j + 1) * B_DV)
        h = accf_ref[:, cols] + accb_ref[:, cols]
        o_ref[0, :, cols] = (_rms(h, ng_ref[...]) * _sigmoid(og_ref[0, :, cols])).astype(o_ref.dtype)


def _mlstm_call(p, conv_w, gate_bias, ng, n_ctx):
    b, t, _ = p.shape
    pairs = B_HEADS // 2
    tri = jnp.stack([_order_mask(CHUNK, False), _order_mask(CHUNK, True)]).astype(BF16)
    return pl.pallas_call(
        functools.partial(_mlstm_kernel, t=t, n_ctx=n_ctx),
        grid=(b, pairs),
        in_specs=[pl.BlockSpec((1, t, LANES), lambda bi, hp: (bi, 0, BLK['b_k'] + hp)),
                  pl.BlockSpec((1, t, LANES), lambda bi, hp: (bi, 0, BLK['b_q'] + hp)),
                  pl.BlockSpec((1, t, 2 * B_DV), lambda bi, hp: (bi, 0, BLK['b_v'] // 2 + hp)),
                  pl.BlockSpec((1, t, 2 * B_DV), lambda bi, hp: (bi, 0, BLK['b_o'] // 2 + hp)),
                  pl.BlockSpec((1, t, LANES), lambda bi, hp: (bi, 0, BLK['misc0'] + hp)),
                  pl.BlockSpec((2, B_CONV, LANES), lambda bi, hp: (0, 0, hp)),
                  pl.BlockSpec((1, 1, LANES), lambda bi, hp: (hp, 0, 0)),
                  pl.BlockSpec((1, LANES), lambda bi, hp: (0, 0)),
                  pl.BlockSpec((2, CHUNK, CHUNK), lambda bi, hp: (0, 0, 0))],
        out_specs=pl.BlockSpec((1, t, 2 * B_DV), lambda bi, hp: (bi, 0, hp)),
        out_shape=jax.ShapeDtypeStruct((b, t, B_HEADS * B_DV), BF16),
        scratch_shapes=[pltpu.VMEM((t, LANES), F32), pltpu.VMEM((t, LANES), F32),
                        pltpu.VMEM((t, 2 * B_DV), F32), pltpu.VMEM((t, 2 * B_DV), F32),
                        pltpu.VMEM((2, 2 * B_DQK, 2 * B_DV), F32)],
        compiler_params=_cp(('parallel', 'parallel')),
        name='mlstm',
    )(p, p, p, p, p, conv_w, gate_bias, ng, tri)


def _rope_tile(x, cos, sin_signed):
    lane = lax.broadcasted_iota(jnp.int32, (1, LANES), 1)
    first = (lane // (C_DH // 4)) % 2 == 0
    partner = jnp.where(first, pltpu.roll(x, LANES - C_DH // 4, 1), pltpu.roll(x, C_DH // 4, 1))
    return x * cos + partner * sin_signed


def _head_norm_tile(x, g, bd):
    sq = x * x
    hi = sq.astype(BF16)
    lo = (sq - hi.astype(F32)).astype(BF16)
    ms = _dot(hi, bd) + _dot(lo, bd)
    return x * lax.rsqrt(ms + EPS) * g


def _half_to(x, src_hi, dst_hi):
    lane = lax.broadcasted_iota(jnp.int32, (1, LANES), 1)
    if src_hi != dst_hi:
        x = pltpu.roll(x, LANES // 2, 1)
    keep = (lane >= LANES // 2) if dst_hi else (lane < LANES // 2)
    return jnp.where(keep, x, 0.0)


def _gqa_prep_kernel(cq_ref, ck_ref, cv_ref, cos_ref, sin_ref, qg_ref, kg_ref, bd_ref,
                     q_ref, k_ref, v_ref):
    cos, sin, bd = cos_ref[...], sin_ref[...], bd_ref[...]
    k = _head_norm_tile(ck_ref[0], kg_ref[...], bd)
    k_ref[0] = _rope_tile(k, cos, sin).astype(BF16)
    v_ref[0] = cv_ref[0].astype(BF16)
    grp = C_HEADS // C_KV_HEADS
    for mt in range(C_HEADS // 2):
        x = _head_norm_tile(cq_ref[0, :, mt * LANES:(mt + 1) * LANES], qg_ref[...], bd)
        x = _rope_tile(x, cos, sin) * (C_DH ** -0.5)
        for half in range(2):
            head = 2 * mt + half
            q_ref[0, head] = _half_to(x, half == 1, head // grp == 1).astype(BF16)


def _gqa_prep_call(p, cos, sin, qg, kg, bd):
    b, t, _ = p.shape
    tm = _pick(t, (256, 128))
    tok = lambda blk, w: pl.BlockSpec((1, tm, w), lambda bi, i: (bi, i, blk))
    tab = pl.BlockSpec((tm, LANES), lambda bi, i: (i, 0))
    vec = pl.BlockSpec((1, LANES), lambda bi, i: (0, 0))
    kv_spec = pl.BlockSpec((1, tm, LANES), lambda bi, i: (bi, i, 0))
    return pl.pallas_call(
        _gqa_prep_kernel,
        grid=(b, t // tm),
        in_specs=[tok(BLK['c_q'] // 4, 4 * LANES), tok(BLK['c_k'], LANES), tok(BLK['c_v'], LANES),
                  tab, tab, vec, vec, pl.BlockSpec((LANES, LANES), lambda bi, i: (0, 0))],
        out_specs=[pl.BlockSpec((1, C_HEADS, tm, LANES), lambda bi, i: (bi, 0, i, 0)), kv_spec, kv_spec],
        out_shape=[jax.ShapeDtypeStruct((b, C_HEADS, t, LANES), BF16),
                   jax.ShapeDtypeStruct((b, t, LANES), BF16),
                   jax.ShapeDtypeStruct((b, t, LANES), BF16)],
        compiler_params=_cp(('parallel', 'parallel')),
        name='gqa_prep',
    )(p, p, p, cos, sin, qg, kg, bd)


def _softmax_pv(q, k, v, scale):
    s = _dot_nt(q, k)
    if scale is not None:
        s = s * scale
    m = jnp.max(s, axis=-1, keepdims=True)
    e = jnp.exp(s - m)
    l = jnp.sum(e, axis=-1, keepdims=True)
    return _dot(e.astype(BF16), v) / l


def _gqa_attn_kernel(q_ref, k_ref, v_ref, o_ref, *, tq, t, n_ctx):
    grp = C_HEADS // C_KV_HEADS
    lane = lax.broadcasted_iota(jnp.int32, (1, LANES), 1)

    def attend(nk):
        k, v = k_ref[0, 0:nk, :], v_ref[0, 0:nk, :]
        for g in range(C_KV_HEADS):
            q = q_ref[0, g * grp:(g + 1) * grp].reshape(grp * tq, LANES)
            o = _softmax_pv(q, k, v, None)
            for pair in range(grp // 2):
                o_even = o[(2 * pair) * tq:(2 * pair + 1) * tq]
                o_odd = o[(2 * pair + 1) * tq:(2 * pair + 2) * tq]
                lo = o_even if g == 0 else pltpu.roll(o_even, LANES // 2, 1)
                hi = pltpu.roll(o_odd, LANES // 2, 1) if g == 0 else o_odd
                col = (g * grp // 2 + pair) * LANES
                o_ref[0, :, col:col + LANES] = jnp.where(lane < LANES // 2, lo, hi).astype(o_ref.dtype)

    is_ctx = pl.program_id(1) * tq < n_ctx
    pl.when(is_ctx)(lambda: attend(n_ctx))
    pl.when(jnp.logical_not(is_ctx))(lambda: attend(t))


def _gqa_attn_call(q, k, v, n_ctx):
    b, _, t, _ = q.shape
    tq = _pick(n_ctx, (256, 128))
    kv_spec = pl.BlockSpec((1, t, LANES), lambda bi, i: (bi, 0, 0))
    return pl.pallas_call(
        functools.partial(_gqa_attn_kernel, tq=tq, t=t, n_ctx=n_ctx),
        grid=(b, t // tq),
        in_specs=[pl.BlockSpec((1, C_HEADS, tq, LANES), lambda bi, i: (bi, 0, i, 0)), kv_spec, kv_spec],
        out_specs=pl.BlockSpec((1, tq, C_HEADS * C_DH), lambda bi, i: (bi, i, 0)),
        out_shape=jax.ShapeDtypeStruct((b, t, C_HEADS * C_DH), BF16),
        compiler_params=_cp(('parallel', 'parallel')),
        name='gqa_attn',
    )(q, k, v)


def _mla_prep_kernel(ckv_ref, misc_ref, cq_ref, cos_ref, sin_ref, kvg_ref, qg_ref,
                     wuk_ref, wuv_ref, wqn_ref, wqr_ref, q_ref, k_ref, v_ref):
    cos, sin = cos_ref[...], sin_ref[...]
    ckv = _rms(ckv_ref[0], kvg_ref[...]).astype(BF16)
    k_nope = _dot(ckv, wuk_ref[...])
    v_ref[0] = _dot(ckv, wuv_ref[...]).astype(BF16)
    k_rope = _half_to(_rope_tile(misc_ref[0], cos, sin), False, False).astype(BF16)
    cq = _rms(cq_ref[0], qg_ref[...]).astype(BF16)
    q_nope = _dot(cq, wqn_ref[...])
    q_rope = _dot(cq, wqr_ref[...])
    for mt in range(D_HEADS // 2):
        qr = _rope_tile(q_rope[:, mt * LANES:(mt + 1) * LANES], cos, sin)
        for half in range(2):
            h = 2 * mt + half
            q_ref[0, h, :, 0:D_NOPE] = q_nope[:, h * D_NOPE:(h + 1) * D_NOPE].astype(BF16)
            q_ref[0, h, :, D_NOPE:2 * D_NOPE] = _half_to(qr, half == 1, False).astype(BF16)
            k_ref[0, h, :, 0:D_NOPE] = k_nope[:, h * D_NOPE:(h + 1) * D_NOPE].astype(BF16)
            k_ref[0, h, :, D_NOPE:2 * D_NOPE] = k_rope


def _mla_prep_call(p, cos, sin, kvg, qg, wuk, wuv, wqn, wqr):
    b, t, _ = p.shape
    tm = _pick(t, (256, 128))
    tok = lambda blk, w: pl.BlockSpec((1, tm, w), lambda bi, i: (bi, i, blk))
    tab = pl.BlockSpec((tm, LANES), lambda bi, i: (i, 0))
    full = lambda a: pl.BlockSpec(a.shape, lambda bi, i: (0,) * a.ndim)
    hd = 2 * D_NOPE
    qk_spec = pl.BlockSpec((1, D_HEADS, tm, hd), lambda bi, i: (bi, 0, i, 0))
    return pl.pallas_call(
        _mla_prep_kernel,
        grid=(b, t // tm),
        in_specs=[tok(BLK['d_ckv'], LANES), tok(BLK['misc0'], LANES), tok(BLK['d_cq'] // 2, 2 * LANES),
                  tab, tab, full(kvg), full(qg), full(wuk), full(wuv), full(wqn), full(wqr)],
        out_specs=[qk_spec, qk_spec, pl.BlockSpec((1, tm, D_HEADS * D_DV), lambda bi, i: (bi, i, 0))],
        out_shape=[jax.ShapeDtypeStruct((b, D_HEADS, t, hd), BF16),
                   jax.ShapeDtypeStruct((b, D_HEADS, t, hd), BF16),
                   jax.ShapeDtypeStruct((b, t, D_HEADS * D_DV), BF16)],
        compiler_params=_cp(('parallel', 'parallel')),
        name='mla_prep',
    )(p, p, p, cos, sin, kvg, qg, wuk, wuv, wqn, wqr)


def _mla_attn_kernel(q_ref, k_ref, v_ref, o_ref, *, tq, t, n_ctx):
    def attend(nk):
        o_ref[0] = _softmax_pv(q_ref[0, 0], k_ref[0, 0, 0:nk, :], v_ref[0, 0:nk, :],
                               (D_NOPE + D_ROPE) ** -0.5).astype(o_ref.dtype)

    is_ctx = pl.program_id(2) * tq < n_ctx
    pl.when(is_ctx)(lambda: attend(n_ctx))
    pl.when(jnp.logical_not(is_ctx))(lambda: attend(t))


def _mla_attn_call(q, k, v, n_ctx):
    b, _, t, hd = q.shape
    tq = _pick(n_ctx, (256, 128))
    return pl.pallas_call(
        functools.partial(_mla_attn_kernel, tq=tq, t=t, n_ctx=n_ctx),
        grid=(b, D_HEADS, t // tq),
        in_specs=[pl.BlockSpec((1, 1, tq, hd), lambda bi, h, i: (bi, h, i, 0)),
                  pl.BlockSpec((1, 1, t, hd), lambda bi, h, i: (bi, h, 0, 0)),
                  pl.BlockSpec((1, t, D_DV), lambda bi, h, i: (bi, 0, h))],
        out_specs=pl.BlockSpec((1, tq, D_DV), lambda bi, h, i: (bi, i, h)),
        out_shape=jax.ShapeDtypeStruct((b, t, D_HEADS * D_DV), BF16),
        compiler_params=_cp(('parallel', 'parallel', 'parallel')),
        name='mla_attn',
    )(q, k, v)


def _merge_kernel(ya_ref, yb_ref, yc_ref, yd_ref, x_ref, modb_ref, modc_ref, g_ref, wg_ref, wb_ref, wo_ref,
                  o_ref, *, tm, n_ctx):
    row0 = pl.program_id(1) * tm
    h = _mod_norm(x_ref, g_ref, modb_ref, modc_ref, row0, tm, n_ctx, 0, 1).astype(BF16)
    acc = None
    for r, y_ref in enumerate((ya_ref, yb_ref, yc_ref, yd_ref)):
        term = _sigmoid(_dot(h, wg_ref[r])) * _dot(y_ref[0], wb_ref[r])
        acc = term if acc is None else acc + term
    z = _dot(acc.astype(BF16), wo_ref[...])
    o_ref[0] = x_ref[0] + _res_gate(modb_ref, modc_ref, row0, tm, n_ctx, 2) * z


def _merge_call(ys, x, modb, modc, g, wg, wb, wo, n_ctx):
    b, t, d = x.shape
    tm = _pick(t, (384, 256, 128))
    y_spec = pl.BlockSpec((1, tm, BRANCH_W), lambda bi, i: (bi, i, 0))
    x_spec = pl.BlockSpec((1, tm, d), lambda bi, i: (bi, i, 0))
    return pl.pallas_call(
        functools.partial(_merge_kernel, tm=tm, n_ctx=n_ctx),
        grid=(b, t // tm),
        in_specs=[y_spec] * 4 + [
            x_spec, pl.BlockSpec((1, 6, d), lambda bi, i: (bi, 0, 0)), pl.BlockSpec((6, d), lambda bi, i: (0, 0)),
            pl.BlockSpec((1, d), lambda bi, i: (0, 0)),
            pl.BlockSpec((N_BRANCH, d, d), lambda bi, i: (0, 0, 0)),
            pl.BlockSpec((N_BRANCH, BRANCH_W, d), lambda bi, i: (0, 0, 0)),
            pl.BlockSpec((d, d), lambda bi, i: (0, 0))],
        out_specs=x_spec,
        out_shape=jax.ShapeDtypeStruct((b, t, d), F32),
        compiler_params=_cp(('parallel', 'parallel')),
        name='merge',
    )(*ys, x, modb, modc, g, wg, wb, wo)


def _ffn_kernel(x_ref, modb_ref, modc_ref, g_ref, w1_ref, w2_ref, o_ref, h_ref, acc_ref, *, tm, n_ctx):
    j = pl.program_id(2)
    row0 = pl.program_id(1) * tm

    @pl.when(j == 0)
    def _():
        h_ref[...] = _mod_norm(x_ref, g_ref, modb_ref, modc_ref, row0, tm, n_ctx, 3, 4).astype(BF16)

    u = jnp.square(jnp.maximum(_dot(h_ref[...], w1_ref[...]), 0.0))
    part = _dot(u.astype(BF16), w2_ref[...])

    @pl.when(j == 0)
    def _():
        acc_ref[...] = part

    @pl.when(j > 0)
    def _():
        acc_ref[...] = acc_ref[...] + part

    @pl.when(j == pl.num_programs(2) - 1)
    def _():
        o_ref[0] = x_ref[0] + _res_gate(modb_ref, modc_ref, row0, tm, n_ctx, 5) * acc_ref[...]


def _ffn_call(x, modb, modc, g, w1, w2, n_ctx):
    b, t, d = x.shape
    tm = _pick(t, (1152, 768, 512, 384, 256, 128))
    tf = 1024
    x_spec = pl.BlockSpec((1, tm, d), lambda bi, i, j: (bi, i, 0))
    return pl.pallas_call(
        functools.partial(_ffn_kernel, tm=tm, n_ctx=n_ctx),
        grid=(b, t // tm, D_FF // tf),
        in_specs=[x_spec, pl.BlockSpec((1, 6, d), lambda bi, i, j: (bi, 0, 0)),
                  pl.BlockSpec((6, d), lambda bi, i, j: (0, 0)), pl.BlockSpec((1, d), lambda bi, i, j: (0, 0)),
                  pl.BlockSpec((d, tf), lambda bi, i, j: (0, j)), pl.BlockSpec((tf, d), lambda bi, i, j: (j, 0))],
        out_specs=x_spec,
        out_shape=jax.ShapeDtypeStruct((b, t, d), F32),
        scratch_shapes=[pltpu.VMEM((tm, d), BF16), pltpu.VMEM((tm, d), F32)],
        compiler_params=_cp(('parallel', 'parallel', 'arbitrary')),
        name='ffn',
    )(x, modb, modc, g, w1, w2)


def _final_kernel(x_ref, g_ref, o_ref):
    o_ref[0] = _rms(x_ref[0], g_ref[...])


def _final_call(x, g, n_ctx):
    b, t, d = x.shape
    tm = _pick(n_ctx, (256, 128))
    off = n_ctx // tm
    return pl.pallas_call(
        _final_kernel,
        grid=(b, (t - n_ctx) // tm),
        in_specs=[pl.BlockSpec((1, tm, d), lambda bi, i: (bi, i + off, 0)), pl.BlockSpec((1, d), lambda bi, i: (0, 0))],
        out_specs=pl.BlockSpec((1, tm, d), lambda bi, i: (bi, i, 0)),
        out_shape=jax.ShapeDtypeStruct((b, t - n_ctx, d), F32),
        compiler_params=_cp(('parallel', 'parallel')),
        name='final_norm',
    )(x, g)


def _pack_w_in(w_in):
    depth, d, _ = w_in.shape
    ref = lambda n: w_in[:, :, _REF_OFF[n][0]:_REF_OFF[n][0] + _REF_OFF[n][1]]
    zeros = lambda w: jnp.zeros((depth, d, w), w_in.dtype)
    gates = ref('b_gates').reshape(depth, d, 4, B_HEADS)
    pair_gates = [gates[:, :, :, 2 * hp:2 * hp + 2].reshape(depth, d, 8) for hp in range(2)]
    misc0 = jnp.concatenate([ref('d_krope'), pair_gates[0], zeros(LANES - 72)], axis=-1)
    misc1 = jnp.concatenate([zeros(64), pair_gates[1], zeros(LANES - 72)], axis=-1)
    parts = [ref('a_i'), ref('a_f_fwd'), ref('a_f_bwd'), ref('a_q'), ref('a_g'), ref('b_v'),
             ref('b_o'), ref('c_q'), ref('b_k'), ref('b_q'), ref('d_cq'), ref('c_k'), ref('c_v'), ref('d_ckv'),
             misc0, misc1, zeros(LANES)]
    packed = jnp.concatenate(parts, axis=-1)
    assert packed.shape[-1] == P_WIDTH
    return packed.astype(BF16)


def _rope_tables(n_lat, n_ctx):
    rows = n_lat // GRID_W
    row = jnp.repeat(jnp.arange(rows, dtype=jnp.int32), GRID_W)
    col = jnp.broadcast_to(jnp.arange(GRID_W, dtype=jnp.int32), (rows, GRID_W)).reshape(-1)
    quarter = C_DH // 4
    inv_freq = ROPE_THETA ** (-jnp.arange(quarter, dtype=F32) / quarter)
    ang_r = row.astype(F32)[:, None] * inv_freq
    ang_c = col.astype(F32)[:, None] * inv_freq
    cr, sr, cc, sc = jnp.cos(ang_r), jnp.sin(ang_r), jnp.cos(ang_c), jnp.sin(ang_c)
    cos = jnp.concatenate([cr, cr, cc, cc], axis=-1)
    sin = jnp.concatenate([-sr, sr, -sc, sc], axis=-1)
    cos = jnp.concatenate([jnp.ones((n_ctx, C_DH), F32), cos], axis=0)
    sin = jnp.concatenate([jnp.zeros((n_ctx, C_DH), F32), sin], axis=0)
    return jnp.tile(cos, (1, 2)), jnp.tile(sin, (1, 2))


def kernel(x, c, ctx, c_ctx, w_ada, b_ada, g_norm1, g_norm2, w_in, b_mlstm_gates, hgrn_lb_logits,
           hgrn_norm_g, mlstm_conv_w, mlstm_norm_g, gqa_q_norm_g, gqa_k_norm_g, mla_q_norm_g,
           mla_kv_norm_g, w_mla_uq, w_mla_uk, w_mla_uv, w_branch, w_out, w_ff1, w_ff2, g_final):
    bsz, n_lat, d = x.shape
    n_ctx = ctx.shape[1]
    depth = w_ada.shape[0]
    assert d == D_MODEL and n_lat % GRID_W == 0 and n_ctx % 128 == 0 and n_lat % 128 == 0

    w_in_p = _pack_w_in(w_in)
    w_ada_b = w_ada.astype(BF16)
    wb_b, wo_b = w_branch.astype(BF16), w_out.astype(BF16)
    g0, gw = _REF_OFF['gates']
    wg_b = w_in[:, :, g0:g0 + gw].reshape(depth, d, N_BRANCH, d).transpose(0, 2, 1, 3).astype(BF16)
    w1_b, w2_b = w_ff1.astype(BF16), w_ff2.astype(BF16)
    wuk_b, wuv_b = w_mla_uk.astype(BF16), w_mla_uv.astype(BF16)
    uq = w_mla_uq.reshape(depth, D_Q_LORA, D_HEADS, D_NOPE + D_ROPE)
    wqn_b = uq[..., :D_NOPE].reshape(depth, D_Q_LORA, D_HEADS * D_NOPE).astype(BF16)
    wqr_b = uq[..., D_NOPE:].reshape(depth, D_Q_LORA, D_HEADS * D_ROPE).astype(BF16)
    cos_t, sin_t = _rope_tables(n_lat, n_ctx)
    half = lax.broadcasted_iota(jnp.int32, (LANES, LANES), 0) // C_DH == lax.broadcasted_iota(
        jnp.int32, (LANES, LANES), 1) // C_DH
    bd = jnp.where(half, 1.0 / C_DH, 0.0).astype(BF16)
    lb = jnp.cumsum(jax.nn.softmax(hgrn_lb_logits.astype(F32), axis=0), axis=0)
    lb = lb - lb[0]
    llb, l1m = jnp.log(lb), jnp.log1p(-lb)
    gb = b_mlstm_gates.reshape(depth, 4, B_HEADS)
    gate_bias = jnp.zeros((depth, 2, 1, LANES), F32)
    for hp in range(2):
        gate_bias = gate_bias.at[:, hp, 0, GATE_LANE0:GATE_LANE0 + 8].set(gb[:, :, 2 * hp:2 * hp + 2].reshape(depth, 8))
    tile2 = lambda g: jnp.tile(g, (1, 2))

    pad = (-(bsz + 1)) % 8
    s_rows = jnp.concatenate([c, c_ctx[None, :], jnp.zeros((pad, d), F32)], axis=0)
    mods = _ada_call(s_rows, w_ada_b, b_ada)

    xa = jnp.concatenate([ctx, x], axis=1)
    for l in range(depth):
        modb = mods[l, :bsz].reshape(bsz, 6, d)
        modc = mods[l, bsz].reshape(6, d)
        p = _inproj_call(xa, modb, modc, g_norm1[l][None, :], w_in_p[l], n_ctx)
        ya = _hgrn_call(p, llb[l], l1m[l], hgrn_norm_g[l][None, :], n_ctx)
        yb = _mlstm_call(p, mlstm_conv_w[l], gate_bias[l], mlstm_norm_g[l][None, :], n_ctx)
        qc, kc, vc = _gqa_prep_call(p, cos_t, sin_t, tile2(gqa_q_norm_g[l][None, :]),
                                    tile2(gqa_k_norm_g[l][None, :]), bd)
        yc = _gqa_attn_call(qc, kc, vc, n_ctx)
        qd, kd, vd = _mla_prep_call(p, cos_t, sin_t, mla_kv_norm_g[l][None, :], mla_q_norm_g[l][None, :],
                                    wuk_b[l], wuv_b[l], wqn_b[l], wqr_b[l])
        yd = _mla_attn_call(qd, kd, vd, n_ctx)
        xa = _merge_call((ya, yb, yc, yd), xa, modb, modc, g_norm1[l][None, :], wg_b[l], wb_b[l], wo_b[l], n_ctx)
        xa = _ffn_call(xa, modb, modc, g_norm2[l][None, :], w1_b[l], w2_b[l], n_ctx)
    return _final_call(xa, g_final[None, :], n_ctx)
```
